```python
import math
import jax, jax.numpy as jnp
from jax import lax
import numpy as np

D_MODEL = 1024
BATCH = 32
SEQ = 256
DEPTH = 2
DEC_BATCH = 8
DEC_SEQ = 4096
PAST_LEN = 256

GRID_W = 64
POS_BASE = 10000.0
D_MIX = D_MODEL
W_GROUP = D_MIX // 4
N_DIR = 2
M_HEADS = 4
M_HD = W_GROUP // M_HEADS
M_CHUNK = 64
POOL_WINDOWS = (2, 4, 8, 16)
POOL_GC = W_GROUP // len(POOL_WINDOWS)
G_HEADS = 4
G_HD = W_GROUP // G_HEADS
G_CHUNK = 128
S_HEADS = 4
S_HD = W_GROUP // S_HEADS
S_GROUPS = 2
S_STATE = 128
S_CONV = 3
S_CHUNK = 64
D_FF = 2816
FFN_CONV = 3
ALPHA = (2 * DEPTH) ** 0.25
BETA = (8 * DEPTH) ** -0.25
LN_EPS = 1e-5
M_COLS = 4 * W_GROUP + 2 * N_DIR * M_HEADS
P_COLS = W_GROUP
G_COLS = 2 * W_GROUP
S_XBC = W_GROUP + 2 * S_GROUPS * S_STATE
S_COLS = W_GROUP + S_XBC + N_DIR * S_HEADS
D_IN = M_COLS + P_COLS + G_COLS + S_COLS

kernel_name = 'hybrid_mlstm_pool_gmlp_ssd_diffusion_step'


def layer_norm(x, g=None, b=None):
    xf = x.astype(jnp.float32)
    mu = jnp.mean(xf, -1, keepdims=True)
    var = jnp.mean(jnp.square(xf - mu), -1, keepdims=True)
    y = (xf - mu) * lax.rsqrt(var + LN_EPS)
    if g is not None:
        y = y * g + b
    return y.astype(x.dtype)


def modulate(x, shift, scale):
    return layer_norm(x) * (1 + scale) + shift


def dwconv_centered(x, w, b):
    k = w.shape[0]
    pad = k // 2
    y = lax.conv_general_dilated(x, w[:, None, :], window_strides=(1,), padding=[(pad, pad)],
                                 dimension_numbers=('NWC', 'WIO', 'NWC'), feature_group_count=x.shape[-1])
    return y + b


def grid_pos_embed(rows, dim):
    f32 = jnp.float32
    quarter = dim // 4
    freq = 1.0 / (POS_BASE ** (jnp.arange(quarter, dtype=f32) / quarter))
    r = jnp.repeat(jnp.arange(rows, dtype=f32), GRID_W)
    col = jnp.tile(jnp.arange(GRID_W, dtype=f32), rows)
    def enc(pos):
        ang = pos[:, None] * freq[None, :]
        return jnp.concatenate([jnp.sin(ang), jnp.cos(ang)], -1)
    return jnp.concatenate([enc(r), enc(col)], -1)


def to_chunks(a, n_chunks, size):
    return jnp.moveaxis(a.reshape(a.shape[0], a.shape[1], n_chunks, size, *a.shape[3:]), 2, 0)


def mlstm_scan(q, k, v, ig, lf, c0, n0, m0):
    f32 = jnp.float32
    B, H, T, D = q.shape
    L = M_CHUNK
    nc = T // L
    causal = jnp.tril(jnp.ones((L, L), dtype=bool))

    def step(carry, inp):
        c, n, m = carry
        qc, kc, vc, ic, fc = inp
        b = jnp.cumsum(fc, -1)
        dmat = jnp.where(causal, b[..., :, None] - b[..., None, :] + ic[..., None, :], -jnp.inf)
        inter = b + m[..., None]
        m_q = jnp.maximum(inter, jnp.max(dmat, -1))
        w_inter = jnp.exp(inter - m_q)
        s = jnp.einsum('bhtd,bhsd->bhts', qc, kc) * jnp.exp(dmat - m_q[..., None])
        num = jnp.einsum('bhts,bhse->bhte', s, vc) + w_inter[..., None] * jnp.einsum('bhtd,bhde->bhte', qc, c)
        den = jnp.sum(s, -1) + w_inter * jnp.einsum('bhtd,bhd->bht', qc, n)
        h = num / jnp.maximum(jnp.abs(den), jnp.exp(-m_q))[..., None]
        b_last = b[..., -1]
        dec_s = b_last[..., None] - b + ic
        m_new = jnp.maximum(b_last + m, jnp.max(dec_s, -1))
        w_s = jnp.exp(dec_s - m_new[..., None])
        w_c = jnp.exp(b_last + m - m_new)
        c_new = w_c[..., None, None] * c + jnp.einsum('bhs,bhsd,bhse->bhde', w_s, kc, vc)
        n_new = w_c[..., None] * n + jnp.einsum('bhs,bhsd->bhd', w_s, kc)
        return (c_new, n_new, m_new), h

    init = (c0.astype(f32), n0.astype(f32), m0.astype(f32))
    xs = tuple(to_chunks(a, nc, L) for a in (q, k, v, ig, lf))
    (c, n, m), hs = lax.scan(step, init, xs)
    return jnp.moveaxis(hs, 0, 2).reshape(B, H, T, D), c, n, m


def mlstm_mixer(cols, b_i, b_f, norm_g, state):
    f32 = jnp.float32
    B, T, _ = cols.shape
    q, k, v, o, gates = jnp.split(cols, [W_GROUP, 2 * W_GROUP, 3 * W_GROUP, 4 * W_GROUP], axis=-1)
    heads = lambda a: a.reshape(B, T, M_HEADS, M_HD).transpose(0, 2, 1, 3).astype(f32)
    q, k, v = heads(q), heads(k) * (M_HD ** -0.5), heads(v)
    gates = gates.astype(f32).reshape(B, T, 2, N_DIR, M_HEADS)
    ig = (gates[:, :, 0] + b_i).transpose(0, 2, 3, 1)
    lf = jax.nn.log_sigmoid(gates[:, :, 1] + b_f).transpose(0, 2, 3, 1)
    c0, n0, m0 = state
    rev = lambda a: jnp.flip(a, axis=2)
    hf, cf, nf, mf = mlstm_scan(q, k, v, ig[:, 0], lf[:, 0], c0[:, 0], n0[:, 0], m0[:, 0])
    hb, cb, nb, mb = mlstm_scan(rev(q), rev(k), rev(v), rev(ig[:, 1]), rev(lf[:, 1]), c0[:, 1], n0[:, 1], m0[:, 1])
    h = hf + rev(hb)
    h = layer_norm(h).transpose(0, 2, 1, 3).reshape(B, T, W_GROUP) * norm_g
    out = jax.nn.sigmoid(o.astype(f32)) * h
    new_state = (jnp.stack([cf, cb], 1), jnp.stack([nf, nb], 1), jnp.stack([mf, mb], 1))
    return out.astype(cols.dtype), new_state


def pool_mixer(x, w_pool, scale):
    f32 = jnp.float32
    B, T, _ = x.shape
    xg = x.reshape(B, T, len(POOL_WINDOWS), POOL_GC)
    csum = jnp.concatenate([jnp.zeros((B, 1) + xg.shape[2:], f32), jnp.cumsum(xg.astype(f32), axis=1)], axis=1)
    t = jnp.arange(T)
    outs = []
    for g, w in enumerate(POOL_WINDOWS):
        lo = jnp.clip(t - w // 2, 0, T)
        hi = jnp.clip(t - w // 2 + w, 0, T)
        cg = csum[:, :, g]
        mean = (cg[:, hi] - cg[:, lo]) / (hi - lo).astype(f32)[:, None]
        outs.append(mean - xg[:, :, g].astype(f32))
    pooled = jnp.stack(outs, 2).astype(x.dtype)
    y = jnp.einsum('btgc,gcd->btgd', pooled, w_pool).reshape(B, T, W_GROUP)
    return y * scale


def gmlp_mixer(cols, w_s, b_s):
    B, T, _ = cols.shape
    u, v = jnp.split(cols, 2, axis=-1)
    v = layer_norm(v)
    vc = v.reshape(B, T // G_CHUNK, G_CHUNK, G_HEADS, G_HD)
    mixed = jnp.einsum('hts,bcshd->bcthd', w_s, vc) + b_s.T[None, None, :, :, None]
    return u * mixed.reshape(B, T, W_GROUP)


def ssd_scan(x, dt, a, bm, cm, h0):
    B, H, T, P = x.shape
    L = S_CHUNK
    nc = T // L
    causal = jnp.tril(jnp.ones((L, L), dtype=bool))

    def step(h, inp):
        xc, dtc, bc, cc = inp
        lc = jnp.cumsum(dtc * a[:, None], axis=-1)
        decay = jnp.exp(jnp.where(causal, lc[..., :, None] - lc[..., None, :], -jnp.inf))
        scores = jnp.einsum('bhtn,bhsn->bhts', cc, bc) * decay * dtc[..., None, :]
        y = jnp.einsum('bhts,bhsp->bhtp', scores, xc) + jnp.exp(lc)[..., None] * jnp.einsum('bhtn,bhpn->bhtp', cc, h)
        w_end = jnp.exp(lc[..., -1:] - lc) * dtc
        h_new = jnp.exp(lc[..., -1])[..., None, None] * h + jnp.einsum('bhs,bhsp,bhsn->bhpn', w_end, xc, bc)
        return h_new, y

    xs = tuple(to_chunks(arr, nc, L) for arr in (x, dt, bm, cm))
    h, ys = lax.scan(step, h0.astype(jnp.float32), xs)
    return jnp.moveaxis(ys, 0, 2).reshape(B, H, T, P), h


def ssd_mixer(cols, conv_w, conv_b, dt_bias, a_log, d_skip, norm_g, state):
    f32 = jnp.float32
    B, T, _ = cols.shape
    z, xbc, dtr = jnp.split(cols, [W_GROUP, W_GROUP + S_XBC], axis=-1)
    xbc = jax.nn.silu(dwconv_centered(xbc, conv_w, conv_b))
    xs, bm, cm = jnp.split(xbc, [W_GROUP, W_GROUP + S_GROUPS * S_STATE], axis=-1)
    xh = xs.reshape(B, T, S_HEADS, S_HD).transpose(0, 2, 1, 3).astype(f32)
    rep = S_HEADS // S_GROUPS
    grp = lambda arr: jnp.repeat(arr.reshape(B, T, S_GROUPS, S_STATE), rep, axis=2).transpose(0, 2, 1, 3).astype(f32)
    bh, ch = grp(bm), grp(cm)
    dt = jax.nn.softplus(dtr.astype(f32).reshape(B, T, N_DIR, S_HEADS) + dt_bias).transpose(0, 2, 3, 1)
    a = -jnp.exp(a_log.astype(f32))
    rev = lambda arr: jnp.flip(arr, axis=2)
    yf, hf = ssd_scan(xh, dt[:, 0], a[0], bh, ch, state[:, 0])
    yb, hb = ssd_scan(rev(xh), rev(dt[:, 1]), a[1], rev(bh), rev(ch), state[:, 1])
    y = yf + rev(yb) + d_skip[:, None, None] * xh
    y = y.transpose(0, 2, 1, 3).reshape(B, T, W_GROUP) * jax.nn.silu(z.astype(f32))
    yg = y.reshape(B, T, S_GROUPS, W_GROUP // S_GROUPS)
    yg = yg * lax.rsqrt(jnp.mean(jnp.square(yg), -1, keepdims=True) + LN_EPS)
    out = yg.reshape(B, T, W_GROUP) * norm_g
    return out.astype(cols.dtype), jnp.stack([hf, hb], 1)


def conv_ffn(h, w_up, conv_w, conv_b, w_down):
    u = jnp.einsum('btd,df->btf', h, w_up)
    u = dwconv_centered(u, conv_w, conv_b)
    g, val = jnp.split(u, 2, axis=-1)
    return jnp.einsum('btf,fd->btd', jax.nn.silu(g) * val, w_down)


def trunk_layer(x, cond, p, m_state, s_state):
    mod = (jnp.dot(cond, p['w_ada']) + p['b_ada'])[:, None, :]
    sh1, sc1, g1, sh2, sc2, g2 = jnp.split(mod, 6, axis=-1)
    h = modulate(x, sh1, sc1)
    cols = jnp.einsum('btd,de->bte', h, p['w_in'])
    mc, pc, gc, sc = jnp.split(cols, [M_COLS, M_COLS + P_COLS, M_COLS + P_COLS + G_COLS], axis=-1)
    y_m, m_new = mlstm_mixer(mc, p['b_igate'], p['b_fgate'], p['mlstm_norm_g'], m_state)
    y_p = pool_mixer(pc, p['w_pool'], p['pool_scale'])
    y_g = gmlp_mixer(gc, p['w_spatial'], p['b_spatial'])
    y_s, s_new = ssd_mixer(sc, p['ssd_conv_w'], p['ssd_conv_b'], p['ssd_dt_bias'], p['ssd_a_log'],
                           p['ssd_d'], p['ssd_norm_g'], s_state)
    mix = jnp.einsum('bte,ed->btd', jnp.concatenate([y_m, y_p, y_g, y_s], -1), p['w_out'])
    x = layer_norm(ALPHA * x + g1 * mix, p['ln1_g'], p['ln1_b'])
    h = modulate(x, sh2, sc2)
    ffn = conv_ffn(h, p['ffn_w_up'], p['ffn_conv_w'], p['ffn_conv_b'], p['ffn_w_down'])
    x = layer_norm(ALPHA * x + g2 * ffn, p['ln2_g'], p['ln2_b'])
    return x, m_new, s_new


def setup_inputs(seed: int = 0) -> dict:
    key = jax.random.key(seed)
    ks = iter(jax.random.split(key, 40))
    f32 = jnp.float32
    def nrm(shape, s=1.0):
        return s * jax.random.normal(next(ks), shape, f32)
    def gain(shape):
        return 1.0 + nrm(shape, 0.1)
    L = DEPTH
    x_prompt = nrm((BATCH, SEQ, D_MODEL))
    x_sample = nrm((DEC_BATCH, DEC_SEQ, D_MODEL))
    state_mlstm_c = nrm((DEC_BATCH, L, N_DIR, M_HEADS, M_HD, M_HD), 0.5)
    state_mlstm_n = nrm((DEC_BATCH, L, N_DIR, M_HEADS, M_HD), 0.5)
    state_mlstm_m = nrm((DEC_BATCH, L, N_DIR, M_HEADS), 0.5)
    state_ssd = nrm((DEC_BATCH, L, N_DIR, S_HEADS, S_HD, S_STATE), 0.5)
    c = nrm((DEC_BATCH, D_MODEL))
    c_ctx = nrm((D_MODEL,))
    w_ada = nrm((L, D_MODEL, 6 * D_MODEL), D_MODEL ** -0.5)
    b_ada = nrm((L, 6 * D_MODEL), 0.02)
    w_in = nrm((L, D_MODEL, D_IN), D_MODEL ** -0.5)
    b_igate = nrm((L, N_DIR, M_HEADS), 0.1)
    b_fgate = jnp.linspace(3.0, 6.0, M_HEADS, dtype=f32) + nrm((L, N_DIR, M_HEADS), 0.1)
    mlstm_norm_g = gain((L, W_GROUP))
    w_pool = nrm((L, len(POOL_WINDOWS), POOL_GC, POOL_GC), POOL_GC ** -0.5)
    pool_scale = gain((L, W_GROUP))
    w_spatial = nrm((L, G_HEADS, G_CHUNK, G_CHUNK), G_CHUNK ** -0.5)
    b_spatial = gain((L, G_HEADS, G_CHUNK))
    ssd_conv_w = nrm((L, S_CONV, S_XBC), S_CONV ** -0.5)
    ssd_conv_b = nrm((L, S_XBC), 0.02)
    dt0 = jnp.exp(jax.random.uniform(next(ks), (L, N_DIR, S_HEADS), f32, math.log(1e-3), math.log(1e-1)))
    ssd_dt_bias = dt0 + jnp.log(-jnp.expm1(-dt0))
    ssd_a_log = jnp.log(jax.random.uniform(next(ks), (L, N_DIR, S_HEADS), f32, 1.0, 16.0))
    ssd_d = gain((L, S_HEADS))
    ssd_norm_g = gain((L, W_GROUP))
    w_out = nrm((L, D_MIX, D_MODEL), D_MIX ** -0.5 * BETA)
    ln1_g = gain((L, D_MODEL))
    ln1_b = nrm((L, D_MODEL), 0.02)
    ffn_w_up = nrm((L, D_MODEL, 2 * D_FF), D_MODEL ** -0.5)
    ffn_conv_w = nrm((L, FFN_CONV, 2 * D_FF), FFN_CONV ** -0.5)
    ffn_conv_b = nrm((L, 2 * D_FF), 0.02)
    ffn_w_down = nrm((L, D_FF, D_MODEL), D_FF ** -0.5 * BETA)
    ln2_g = gain((L, D_MODEL))
    ln2_b = nrm((L, D_MODEL), 0.02)
    return {'x_prompt': x_prompt, 'x_sample': x_sample, 'state_mlstm_c': state_mlstm_c,
            'state_mlstm_n': state_mlstm_n, 'state_mlstm_m': state_mlstm_m, 'state_ssd': state_ssd,
            'c': c, 'c_ctx': c_ctx, 'w_ada': w_ada, 'b_ada': b_ada, 'w_in': w_in, 'b_igate': b_igate,
            'b_fgate': b_fgate, 'mlstm_norm_g': mlstm_norm_g, 'w_pool': w_pool, 'pool_scale': pool_scale,
            'w_spatial': w_spatial, 'b_spatial': b_spatial, 'ssd_conv_w': ssd_conv_w, 'ssd_conv_b': ssd_conv_b,
            'ssd_dt_bias': ssd_dt_bias, 'ssd_a_log': ssd_a_log, 'ssd_d': ssd_d, 'ssd_norm_g': ssd_norm_g,
            'w_out': w_out, 'ln1_g': ln1_g, 'ln1_b': ln1_b, 'ffn_w_up': ffn_w_up, 'ffn_conv_w': ffn_conv_w,
            'ffn_conv_b': ffn_conv_b, 'ffn_w_down': ffn_w_down, 'ln2_g': ln2_g, 'ln2_b': ln2_b}


def reference(x_prompt, x_sample, state_mlstm_c, state_mlstm_n, state_mlstm_m, state_ssd, c, c_ctx,
              w_ada, b_ada, w_in, b_igate, b_fgate, mlstm_norm_g, w_pool, pool_scale, w_spatial, b_spatial,
              ssd_conv_w, ssd_conv_b, ssd_dt_bias, ssd_a_log, ssd_d, ssd_norm_g, w_out, ln1_g, ln1_b,
              ffn_w_up, ffn_conv_w, ffn_conv_b, ffn_w_down, ln2_g, ln2_b):
    f32 = jnp.float32
    n_ctx = x_prompt.shape[0]
    t_lat = x_sample.shape[1]
    rows = t_lat // GRID_W
    y_p = x_prompt
    y_s = x_sample + grid_pos_embed(rows, D_MODEL).astype(x_sample.dtype)[None]
    cond_ctx = jax.nn.silu(c_ctx)[None, :]
    cond_lat = jax.nn.silu(c)
    zero_m = (jnp.zeros((n_ctx, N_DIR, M_HEADS, M_HD, M_HD), f32),
              jnp.zeros((n_ctx, N_DIR, M_HEADS, M_HD), f32),
              jnp.zeros((n_ctx, N_DIR, M_HEADS), f32))
    zero_s = jnp.zeros((n_ctx, N_DIR, S_HEADS, S_HD, S_STATE), f32)
    new_c, new_n, new_m, new_s = [], [], [], []
    for l in range(DEPTH):
        p = {'w_ada': w_ada[l], 'b_ada': b_ada[l], 'w_in': w_in[l], 'b_igate': b_igate[l],
             'b_fgate': b_fgate[l], 'mlstm_norm_g': mlstm_norm_g[l], 'w_pool': w_pool[l],
             'pool_scale': pool_scale[l], 'w_spatial': w_spatial[l], 'b_spatial': b_spatial[l],
             'ssd_conv_w': ssd_conv_w[l], 'ssd_conv_b': ssd_conv_b[l], 'ssd_dt_bias': ssd_dt_bias[l],
             'ssd_a_log': ssd_a_log[l], 'ssd_d': ssd_d[l], 'ssd_norm_g': ssd_norm_g[l], 'w_out': w_out[l],
             'ln1_g': ln1_g[l], 'ln1_b': ln1_b[l], 'ffn_w_up': ffn_w_up[l], 'ffn_conv_w': ffn_conv_w[l],
             'ffn_conv_b': ffn_conv_b[l], 'ffn_w_down': ffn_w_down[l], 'ln2_g': ln2_g[l], 'ln2_b': ln2_b[l]}
        y_p, (mc, mn, mm), ss = trunk_layer(y_p, cond_ctx, p, zero_m, zero_s)
        new_c.append(mc)
        new_n.append(mn)
        new_m.append(mm)
        new_s.append(ss)
        y_s, _, _ = trunk_layer(y_s, cond_lat, p,
                                (state_mlstm_c[:, l], state_mlstm_n[:, l], state_mlstm_m[:, l]),
                                state_ssd[:, l])
    return (y_p, y_s, jnp.stack(new_c, 1), jnp.stack(new_n, 1), jnp.stack(new_m, 1), jnp.stack(new_s, 1))
```

```python
import functools
import math

import numpy as np
import jax
import jax.numpy as jnp
from jax import lax
from jax.experimental import pallas as pl
from jax.experimental.pallas import tpu as pltpu

F32 = jnp.float32
BF16 = jnp.bfloat16

D_MODEL = 1024
DEPTH = 2
GRID_W = 64
POS_BASE = 10000.0
W_GROUP = 256
N_DIR = 2
HEADS = 4
HD = 64
CHUNK = 64
POOL_WINDOWS = (2, 4, 8, 16)
G_CHUNK = 128
S_STATE = 128
S_XBC = 768
D_FF = 2816
FF_CHUNK = 256
ALPHA = (2 * DEPTH) ** 0.25
LN_EPS = 1e-5
N_COND = 16
CTX_COND = 8
LANES = 128
SUBLANES = 8
QKV_W = 768
MISC_W = 1280
IN_W = QKV_W + MISC_W + S_XBC + LANES
VMEM_LIMIT = 56 * 1024 * 1024


def _dot(a, b):
    return jnp.dot(a, b, preferred_element_type=F32)


def _dot_nt(a, b):
    return lax.dot_general(a, b, (((1,), (1,)), ((), ())), preferred_element_type=F32)


def _dot_tn(a, b):
    return lax.dot_general(a, b, (((0,), (0,)), ((), ())), preferred_element_type=F32)


def _split3(x):
    hi = x.astype(BF16)
    r1 = x - hi.astype(F32)
    mid = r1.astype(BF16)
    lo = (r1 - mid.astype(F32)).astype(BF16)
    return hi, mid, lo


def _dot_sel(x, sel):
    hi, mid, lo = _split3(x)
    return _dot(hi, sel) + _dot(mid, sel) + _dot(lo, sel)


def _dot_sel2(x, sel):
    hi = x.astype(BF16)
    lo = (x - hi.astype(F32)).astype(BF16)
    return _dot(hi, sel) + _dot(lo, sel)


def _ln(x):
    mu = jnp.mean(x, -1, keepdims=True)
    xc = x - mu
    var = jnp.mean(xc * xc, -1, keepdims=True)
    return xc * lax.rsqrt(var + LN_EPS)


def _sigmoid(x):
    return 1.0 / (1.0 + jnp.exp(-x))


def _silu(x):
    return x * _sigmoid(x)


def _softplus(x):
    return jnp.maximum(x, 0.0) + jnp.log1p(jnp.exp(-jnp.abs(x)))


def _log_sigmoid(x):
    return jnp.minimum(x, 0.0) - jnp.log1p(jnp.exp(-jnp.abs(x)))


def _iota(shape, dim):
    return lax.broadcasted_iota(jnp.int32, shape, dim)


def _seg_scan(x, op, ident, reverse):
    rows = x.shape[0]
    pos = _iota(x.shape, 0) % CHUNK
    k = 1
    while k < CHUNK:
        if reverse:
            shifted = pltpu.roll(x, rows - k, 0)
            valid = pos < CHUNK - k
        else:
            shifted = pltpu.roll(x, k, 0)
            valid = pos >= k
        x = op(x, jnp.where(valid, shifted, ident))
        k *= 2
    return x


def _diag_row(tile, diag):
    return jnp.sum(jnp.where(diag, tile, 0.0), axis=0, keepdims=True)


def _tile4(x):
    return jnp.concatenate([x, x, x, x], axis=0)


def _ada_kernel(cond_ref, w_ref, b_ref, o_ref):
    s = _silu(cond_ref[...]).astype(BF16)
    o_ref[0] = _dot(s, w_ref[0].astype(BF16)) + b_ref[0]


def _ada_call(cond, w_ada, b_ada):
    n_out = w_ada.shape[-1]
    tn = 1536
    return pl.pallas_call(
        _ada_kernel,
        grid=(DEPTH, n_out // tn),
        in_specs=[
            pl.BlockSpec((N_COND, D_MODEL), lambda l, n: (0, 0)),
            pl.BlockSpec((1, D_MODEL, tn), lambda l, n: (l, 0, n)),
            pl.BlockSpec((1, 1, tn), lambda l, n: (l, 0, n)),
        ],
        out_specs=pl.BlockSpec((1, N_COND, tn), lambda l, n: (l, 0, n)),
        out_shape=jax.ShapeDtypeStruct((DEPTH, N_COND, n_out), F32),
        compiler_params=pltpu.CompilerParams(
            dimension_semantics=("arbitrary", "arbitrary"), vmem_limit_bytes=VMEM_LIMIT),
        name="ada_mod",
    )(cond, w_ada, b_ada.reshape(DEPTH, 1, n_out))


def _inproj_kernel(*refs, has_pos):
    if has_pos:
        x_ref, pos_ref, mod_ref, w_ref, qkv_ref, misc_ref, xbc_ref, gates_ref, xres_ref = refs
        x = x_ref[...] + pos_ref[...]
        xres_ref[...] = x
    else:
        x_ref, mod_ref, w_ref, qkv_ref, misc_ref, xbc_ref, gates_ref = refs
        x = x_ref[...]
    shift = mod_ref[0, 0:1, :]
    scale = mod_ref[0, 1:2, :]
    h = (_ln(x) * (1.0 + scale) + shift).astype(BF16)
    qkv = _dot(h, w_ref[:, 0:QKV_W])
    qkv_ref[:, 0:256] = qkv[:, 0:256].astype(BF16)
    qkv_ref[:, 256:512] = (qkv[:, 256:512] * (HD ** -0.5)).astype(BF16)
    qkv_ref[:, 512:768] = qkv[:, 512:768].astype(BF16)
    misc_ref[...] = _dot(h, w_ref[:, QKV_W:QKV_W + MISC_W])
    xbc_ref[...] = _dot(h, w_ref[:, QKV_W + MISC_W:QKV_W + MISC_W + S_XBC])
    gates_ref[...] = _dot(h, w_ref[:, QKV_W + MISC_W + S_XBC:IN_W])


def _inproj_call(x, pos, mod_l, w_in_p, seq, rows_blk, cond_of_tile):
    rows = x.shape[0]
    nt = rows // rows_blk
    per_seq = seq // rows_blk
    has_pos = pos is not None
    row_spec = lambda w: pl.BlockSpec((rows_blk, w), lambda i: (i, 0))
    in_specs = [row_spec(D_MODEL)]
    args = [x]
    if has_pos:
        in_specs.append(pl.BlockSpec((rows_blk, D_MODEL), lambda i: (i % per_seq, 0)))
        args.append(pos)
    in_specs += [
        pl.BlockSpec((1, 6, D_MODEL), lambda i: (cond_of_tile(i), 0, 0)),
        pl.BlockSpec((D_MODEL, IN_W), lambda i: (0, 0)),
    ]
    args += [mod_l, w_in_p]
    out_specs = [row_spec(QKV_W), row_spec(MISC_W), row_spec(S_XBC), row_spec(LANES)]
    out_shape = [jax.ShapeDtypeStruct((rows, QKV_W), BF16), jax.ShapeDtypeStruct((rows, MISC_W), F32),
                 jax.ShapeDtypeStruct((rows, S_XBC), F32), jax.ShapeDtypeStruct((rows, LANES), F32)]
    if has_pos:
        out_specs.append(row_spec(D_MODEL))
        out_shape.append(jax.ShapeDtypeStruct((rows, D_MODEL), F32))
    return pl.pallas_call(
        functools.partial(_inproj_kernel, has_pos=has_pos),
        grid=(nt,), in_specs=in_specs, out_specs=out_specs, out_shape=out_shape,
        compiler_params=pltpu.CompilerParams(dimension_semantics=("arbitrary",), vmem_limit_bytes=VMEM_LIMIT),
        name="inproj",
    )(*args)


def _mlstm_prepare(gates, bias_row, sel, m_prev, reverse, d, nb, ea_ref, ew_ref, eemq_ref, ews_ref,
                   grow_ref, wc_ref, diag):
    ex = _dot_sel(gates + bias_row, sel)
    ig = ex[:, 0:256]
    lf = _log_sigmoid(ex[:, 256:512])
    bcs = _seg_scan(lf, jnp.add, 0.0, reverse)
    g = ig - bcs
    gmax = _seg_scan(g, jnp.maximum, -jnp.inf, reverse)
    order = range(nb - 1, -1, -1) if reverse else range(nb)
    for c in order:
        lo = c * CHUNK
        last = lo if reverse else lo + CHUNK - 1
        m_last = jnp.maximum(m_prev, gmax[last:last + 1, :])
        m_q = jnp.maximum(m_prev, gmax[lo:lo + CHUNK, :])
        ea_ref[d, lo:lo + CHUNK, :] = -m_q
        ew_ref[d, lo:lo + CHUNK, :] = jnp.exp(m_prev - m_q)
        eemq_ref[d, lo:lo + CHUNK, :] = jnp.exp(-(bcs[lo:lo + CHUNK, :] + m_q))
        ews_ref[d, lo:lo + CHUNK, :] = jnp.exp(g[lo:lo + CHUNK, :] - m_last)
        grow_ref[d, c:c + 1, :] = _diag_row(g[lo:lo + CHUNK, :], diag)
        wc_ref[d, c:c + 1, :] = jnp.exp(m_prev - m_last)
        m_prev = bcs[last:last + 1, :] + m_last
    return m_prev


def _mlstm_kernel(qkvf_ref, qkvb_ref, gf_ref, gb_ref, bias_ref, sel_ref, c0_ref, m0_ref,
                  hf_ref, hb_ref, cout_ref, mout_ref,
                  caug_ref, m_ref, ea_ref, ew_ref, eemq_ref, ews_ref, grow_ref, wc_ref, *, nb):
    j = pl.program_id(1)
    nblk = pl.num_programs(1)

    @pl.when(j == 0)
    def _():
        caug_ref[...] = c0_ref[0]
        m_ref[...] = m0_ref[0]

    lane = _iota((CHUNK, 256), 1)
    row = _iota((CHUNK, 256), 0)
    diag = (lane % CHUNK) == row
    causal_f = (lane % CHUNK) <= row
    causal_b = (lane % CHUNK) >= row
    bd = (_iota((256, 256), 0) // HD) == (_iota((256, 256), 1) // HD)
    bd2 = jnp.concatenate([bd, bd], axis=1)
    ones_bd = jnp.where(bd, 1.0, 0.0).astype(BF16)
    ones_rows = jnp.ones((CHUNK, 256), BF16)

    qkv_refs = (qkvf_ref, qkvb_ref)
    g_refs = (gf_ref, gb_ref)
    h_refs = (hf_ref, hb_ref)
    for d in range(N_DIR):
        m_new = _mlstm_prepare(g_refs[d][...], bias_ref[...], sel_ref[:, d * 512:(d + 1) * 512],
                               m_ref[d], bool(d), d, nb, ea_ref, ew_ref, eemq_ref, ews_ref,
                               grow_ref, wc_ref, diag)
        m_ref[d] = m_new

    def chunk(i, carry):
        for d in range(N_DIR):
            c = (nb - 1 - i) if d else i
            r0 = pl.multiple_of(c * CHUNK, CHUNK)
            rows = pl.ds(r0, CHUNK)
            q = qkv_refs[d][rows, 0:256]
            k = qkv_refs[d][rows, 256:512]
            v = qkv_refs[d][rows, 512:768]
            kbd = jnp.where(bd, _tile4(k), jnp.zeros((), BF16))
            s = _dot_nt(q, kbd)
            expo = ea_ref[d, rows, :] + grow_ref[d, pl.ds(c, 1), :]
            dmat = jnp.exp(jnp.where(causal_b if d else causal_f, expo, -jnp.inf))
            p = (s * dmat).astype(BF16)
            vaug = jnp.concatenate([jnp.where(bd, _tile4(v), jnp.zeros((), BF16)), ones_bd], axis=1)
            nd = _dot(p, vaug)
            qc = _dot(q, caug_ref[d].astype(BF16))
            ew = ew_ref[d, rows, :]
            num = nd[:, 0:256] + ew * qc[:, 0:256]
            den = nd[:, 256:512] + ew * qc[:, 256:512]
            h_refs[d][rows, :] = num / jnp.maximum(jnp.abs(den), eemq_ref[d, rows, :])
            kw = (k.astype(F32) * ews_ref[d, rows, :]).astype(BF16)
            u = _dot_tn(kw, jnp.concatenate([v, ones_rows], axis=1))
            wc = wc_ref[d, pl.ds(c, 1), :]
            caug_ref[d] = jnp.concatenate([wc, wc], axis=1) * caug_ref[d] + jnp.where(bd2, u, 0.0)
        return carry

    lax.fori_loop(0, nb, chunk, 0)

    @pl.when(j == nblk - 1)
    def _():
        cout_ref[0] = caug_ref[...]
        mout_ref[0] = m_ref[...]


def _mlstm_call(qkv, gates, bias_row, sel, c0, m0, batch, seq, rows_blk):
    rows = qkv.shape[0]
    nblk = seq // rows_blk
    nb = rows_blk // CHUNK
    fwd = lambda w: pl.BlockSpec((rows_blk, w), lambda b, j: (b * nblk + j, 0))
    bwd = lambda w: pl.BlockSpec((rows_blk, w), lambda b, j: (b * nblk + nblk - 1 - j, 0))
    exp_scr = pltpu.VMEM((N_DIR, rows_blk, 256), F32)
    row_scr = pltpu.VMEM((N_DIR, SUBLANES, 256), F32)
    return pl.pallas_call(
        functools.partial(_mlstm_kernel, nb=nb),
        grid=(batch, nblk),
        in_specs=[fwd(QKV_W), bwd(QKV_W), fwd(LANES), bwd(LANES),
                  pl.BlockSpec((1, LANES), lambda b, j: (0, 0)),
                  pl.BlockSpec((LANES, 1024), lambda b, j: (0, 0)),
                  pl.BlockSpec((1, N_DIR, 256, 512), lambda b, j: (b, 0, 0, 0)),
                  pl.BlockSpec((1, N_DIR, 1, 256), lambda b, j: (b, 0, 0, 0))],
        out_specs=[fwd(256), bwd(256),
                   pl.BlockSpec((1, N_DIR, 256, 512), lambda b, j: (b, 0, 0, 0)),
                   pl.BlockSpec((1, N_DIR, 1, 256), lambda b, j: (b, 0, 0, 0))],
        out_shape=[jax.ShapeDtypeStruct((rows, 256), F32), jax.ShapeDtypeStruct((rows, 256), F32),
                   jax.ShapeDtypeStruct((batch, N_DIR, 256, 512), F32),
                   jax.ShapeDtypeStruct((batch, N_DIR, 1, 256), F32)],
        scratch_shapes=[pltpu.VMEM((N_DIR, 256, 512), F32), pltpu.VMEM((N_DIR, 1, 256), F32),
                        exp_scr, exp_scr, exp_scr, exp_scr, row_scr, row_scr],
        compiler_params=pltpu.CompilerParams(
            dimension_semantics=("arbitrary", "arbitrary"), vmem_limit_bytes=VMEM_LIMIT),
        name="mlstm",
    )(qkv, qkv, gates, gates, bias_row, sel, c0, m0)


def _ssd_prepare(xbc_ref, prev_ref, next_ref, first, last_blk, gates, cw_ref, cb_ref, dtb_ref, sel, a_row,
                 reverse, d, nb, xs_ref, bc_ref, elc_ref, eel_ref, ewend_ref, grow_ref, dtrow_ref, da_ref, diag):
    x = xbc_ref[...]
    rows = x.shape[0]
    rid = _iota(x.shape, 0)
    xp = jnp.where(first, 0.0, prev_ref[SUBLANES - 1:SUBLANES, :])
    xn = jnp.where(last_blk, 0.0, next_ref[0:1, :])
    x_prev = jnp.where(rid == 0, xp, pltpu.roll(x, 1, 0))
    x_next = jnp.where(rid == rows - 1, xn, pltpu.roll(x, rows - 1, 0))
    conv = cw_ref[0:1, :] * x_prev + cw_ref[1:2, :] * x + cw_ref[2:3, :] * x_next + cb_ref[...]
    act = _silu(conv)
    xs_ref[d] = act[:, 0:256]
    bc_ref[d] = act[:, 256:768].astype(BF16)
    dt = _softplus(_dot_sel(gates + dtb_ref[...], sel))
    lc = _seg_scan(dt * a_row, jnp.add, 0.0, reverse)
    elc_ref[d] = lc
    eel_ref[d] = jnp.exp(lc)
    for c in range(nb):
        lo = c * CHUNK
        last = lo if reverse else lo + CHUNK - 1
        lc_last = lc[last:last + 1, :]
        da_ref[d, c:c + 1, :] = jnp.exp(lc_last)
        ewend_ref[d, lo:lo + CHUNK, :] = jnp.exp(lc_last - lc[lo:lo + CHUNK, :]) * dt[lo:lo + CHUNK, :]
        grow_ref[d, c:c + 1, :] = _diag_row(-lc[lo:lo + CHUNK, :], diag)
        dtrow_ref[d, c:c + 1, :] = _diag_row(dt[lo:lo + CHUNK, :], diag)


def _ssd_kernel(xf_ref, xfp_ref, xfn_ref, xb_ref, xbp_ref, xbn_ref, gf_ref, gb_ref,
                cw_ref, cb_ref, dtb_ref, sel_ref, alog_ref, dskip_ref, h0_ref,
                yf_ref, yb_ref, hout_ref,
                h_ref, xs_ref, bc_ref, elc_ref, eel_ref, ewend_ref, grow_ref, dtrow_ref, da_ref, *, nb):
    j = pl.program_id(1)
    nblk = pl.num_programs(1)

    @pl.when(j == 0)
    def _():
        h_ref[...] = h0_ref[0]

    lane = _iota((CHUNK, 256), 1)
    row = _iota((CHUNK, 256), 0)
    diag = (lane % CHUNK) == row
    causal_f = (lane % CHUNK) <= row
    causal_b = (lane % CHUNK) >= row
    r256 = _iota((256, 256), 0)
    c256 = _iota((256, 256), 1)
    bd = (r256 // HD) == (c256 // HD)
    gmask_b = (r256 // HD // 2) == (c256 // S_STATE)
    gmask_h = (r256 // S_STATE) == (c256 // HD // 2)

    x_refs = ((xf_ref, xfp_ref, xfn_ref), (xb_ref, xbp_ref, xbn_ref))
    g_refs = (gf_ref, gb_ref)
    y_refs = (yf_ref, yb_ref)
    for d in range(N_DIR):
        blk = (nblk - 1 - j) if d else j
        _ssd_prepare(x_refs[d][0], x_refs[d][1], x_refs[d][2], blk == 0, blk == nblk - 1, g_refs[d][...],
                     cw_ref, cb_ref, dtb_ref, sel_ref[:, d * 256:(d + 1) * 256], -jnp.exp(alog_ref[d]),
                     bool(d), d, nb, xs_ref, bc_ref, elc_ref, eel_ref, ewend_ref, grow_ref, dtrow_ref, da_ref, diag)

    def chunk(i, carry):
        for d in range(N_DIR):
            c = (nb - 1 - i) if d else i
            r0 = pl.multiple_of(c * CHUNK, CHUNK)
            rows = pl.ds(r0, CHUNK)
            xs = xs_ref[d, rows, :]
            bm = bc_ref[d, rows, 0:256]
            cm = bc_ref[d, rows, 256:512]
            bbd = jnp.where(gmask_b, _tile4(bm), jnp.zeros((), BF16))
            cb = _dot_nt(cm, bbd)
            expo = elc_ref[d, rows, :] + grow_ref[d, pl.ds(c, 1), :]
            dmat = jnp.exp(jnp.where(causal_b if d else causal_f, expo, -jnp.inf)) * dtrow_ref[d, pl.ds(c, 1), :]
            p = (cb * dmat).astype(BF16)
            xbd = jnp.where(bd, _tile4(xs.astype(BF16)), jnp.zeros((), BF16))
            y = _dot(p, xbd) + eel_ref[d, rows, :] * _dot(cm, h_ref[d].astype(BF16))
            if d == 0:
                y = y + dskip_ref[...] * xs
            y_refs[d][rows, :] = y
            xw = (xs * ewend_ref[d, rows, :]).astype(BF16)
            u = _dot_tn(bm, xw)
            h_ref[d] = da_ref[d, pl.ds(c, 1), :] * h_ref[d] + jnp.where(gmask_h, u, 0.0)
        return carry

    lax.fori_loop(0, nb, chunk, 0)

    @pl.when(j == nblk - 1)
    def _():
        hout_ref[0] = h_ref[...]


def _ssd_call(xbc, gates, conv_w, conv_b, dtb_row, sel, alog_exp, dskip_exp, h0, batch, seq, rows_blk):
    rows = xbc.shape[0]
    nblk = seq // rows_blk
    nb = rows_blk // CHUNK
    sub_per_blk = rows_blk // SUBLANES
    n_sub = rows // SUBLANES
    fblk = lambda b, j: b * nblk + j
    bblk = lambda b, j: b * nblk + nblk - 1 - j
    main = lambda f, w: pl.BlockSpec((rows_blk, w), lambda b, j: (f(b, j), 0))
    prev = lambda f: pl.BlockSpec((SUBLANES, S_XBC), lambda b, j: (jnp.maximum(f(b, j) * sub_per_blk - 1, 0), 0))
    nxt = lambda f: pl.BlockSpec((SUBLANES, S_XBC),
                                 lambda b, j: (jnp.minimum((f(b, j) + 1) * sub_per_blk, n_sub - 1), 0))
    const = lambda shape: pl.BlockSpec(shape, lambda b, j: tuple(0 for _ in shape))
    exp_scr = pltpu.VMEM((N_DIR, rows_blk, 256), F32)
    row_scr = pltpu.VMEM((N_DIR, SUBLANES, 256), F32)
    return pl.pallas_call(
        functools.partial(_ssd_kernel, nb=nb),
        grid=(batch, nblk),
        in_specs=[main(fblk, S_XBC), prev(fblk), nxt(fblk), main(bblk, S_XBC), prev(bblk), nxt(bblk),
                  main(fblk, LANES), main(bblk, LANES),
                  const((3, S_XBC)), const((1, S_XBC)), const((1, LANES)), const((LANES, 512)),
                  const((N_DIR, 1, 256)), const((1, 256)),
                  pl.BlockSpec((1, N_DIR, 256, 256), lambda b, j: (b, 0, 0, 0))],
        out_specs=[main(fblk, 256), main(bblk, 256),
                   pl.BlockSpec((1, N_DIR, 256, 256), lambda b, j: (b, 0, 0, 0))],
        out_shape=[jax.ShapeDtypeStruct((rows, 256), F32), jax.ShapeDtypeStruct((rows, 256), F32),
                   jax.ShapeDtypeStruct((batch, N_DIR, 256, 256), F32)],
        scratch_shapes=[pltpu.VMEM((N_DIR, 256, 256), F32), exp_scr,
                        pltpu.VMEM((N_DIR, rows_blk, 512), BF16), exp_scr, exp_scr, exp_scr,
                        row_scr, row_scr, row_scr],
        compiler_params=pltpu.CompilerParams(
            dimension_semantics=("arbitrary", "arbitrary"), vmem_limit_bytes=VMEM_LIMIT),
        name="ssd",
    )(xbc, xbc, xbc, xbc, xbc, xbc, gates, gates, conv_w, conv_b, dtb_row, sel, alog_exp, dskip_exp, h0)


def _mix_kernel(x_ref, hf_ref, hb_ref, misc_ref, pprev_ref, pnext_ref, yf_ref, yb_ref, mod_ref,
                wout_ref, wpool_ref, pscale_ref, mnorm_ref, snorm_ref, wsp_ref, bsp_ref, ln_g_ref, ln_b_ref,
                mean_sel_ref, o_ref, cat_ref, pad_ref, *, seq, per_seq):
    rows = x_ref.shape[0]
    i = pl.program_id(0)
    hs = hf_ref[...] + hb_ref[...]
    mu = _dot_sel2(hs, mean_sel_ref[...])
    hc = hs - mu
    var = _dot_sel2(hc * hc, mean_sel_ref[...])
    y_m = _sigmoid(misc_ref[:, 0:256]) * (hc * lax.rsqrt(var + LN_EPS) * mnorm_ref[...])
    cat_ref[:, 0:256] = y_m.astype(BF16)
    blk = i % per_seq
    xp = misc_ref[:, 256:512]
    pad_ref[0:SUBLANES, :] = jnp.where(blk == 0, 0.0, pprev_ref[...])
    pad_ref[SUBLANES:SUBLANES + rows, :] = xp
    pad_ref[SUBLANES + rows:2 * SUBLANES + rows, :] = jnp.where(blk == per_seq - 1, 0.0, pnext_ref[...])
    lane = _iota((rows, 256), 1)
    win = jnp.left_shift(2, lane // HD)
    half = win // 2
    wsum = jnp.zeros((rows, 256), F32)
    xpad = pad_ref[...]
    n_pad = rows + 2 * SUBLANES
    for k in range(-SUBLANES, SUBLANES):
        inside = (k >= -half) & (k < half)
        shifted = xpad if k == 0 else pltpu.roll(xpad, (-k) % n_pad, 0)
        wsum = wsum + jnp.where(inside, shifted[SUBLANES:SUBLANES + rows, :], 0.0)
    t = blk * rows + _iota((rows, 256), 0)
    cnt = jnp.clip(t - half + win, 0, seq) - jnp.clip(t - half, 0, seq)
    pooled = wsum / cnt.astype(F32) - xp
    y_p = _dot(pooled.astype(BF16), wpool_ref[...]) * pscale_ref[...]
    cat_ref[:, 256:512] = y_p.astype(BF16)
    vn = _ln(misc_ref[:, 768:1024]).astype(BF16)
    bd = (_iota((512, 256), 0) // G_CHUNK) == (_iota((512, 256), 1) // HD)
    for c in range(rows // G_CHUNK):
        r = slice(c * G_CHUNK, (c + 1) * G_CHUNK)
        vbd = jnp.where(bd, _tile4(vn[r, :]), jnp.zeros((), BF16))
        mixed = _dot(wsp_ref[...], vbd) + bsp_ref[...]
        cat_ref[r, 512:768] = (misc_ref[r, 512:768] * mixed).astype(BF16)
    y = (yf_ref[...] + yb_ref[...]) * _silu(misc_ref[:, 1024:1280])
    for g in range(2):
        yg = y[:, g * LANES:(g + 1) * LANES]
        yg = yg * lax.rsqrt(jnp.mean(yg * yg, -1, keepdims=True) + LN_EPS)
        cat_ref[:, 768 + g * LANES:768 + (g + 1) * LANES] = (yg * snorm_ref[:, g * LANES:(g + 1) * LANES]).astype(BF16)
    mix = _dot(cat_ref[...], wout_ref[...])
    gate = mod_ref[0, 2:3, :]
    o_ref[...] = _ln(ALPHA * x_ref[...] + gate * mix) * ln_g_ref[...] + ln_b_ref[...]


def _mix_call(x, hf, hb, misc, yf, yb, mod_l, w_out, w_pool_bd, pool_scale, mnorm, snorm, wsp_cat, bsp_exp,
              ln_g, ln_b, mean_sel, seq, rows_blk, cond_of_tile):
    rows = x.shape[0]
    nt = rows // rows_blk
    per_seq = seq // rows_blk
    sub_per_blk = rows_blk // SUBLANES
    n_sub = rows // SUBLANES
    row_spec = lambda w: pl.BlockSpec((rows_blk, w), lambda i: (i, 0))
    const = lambda shape: pl.BlockSpec(shape, lambda i: tuple(0 for _ in shape))
    return pl.pallas_call(
        functools.partial(_mix_kernel, seq=seq, per_seq=per_seq),
        grid=(nt,),
        in_specs=[row_spec(D_MODEL), row_spec(256), row_spec(256), row_spec(MISC_W),
                  pl.BlockSpec((SUBLANES, 256), lambda i: (jnp.maximum(i * sub_per_blk - 1, 0), 1)),
                  pl.BlockSpec((SUBLANES, 256), lambda i: (jnp.minimum((i + 1) * sub_per_blk, n_sub - 1), 1)),
                  row_spec(256), row_spec(256),
                  pl.BlockSpec((1, 6, D_MODEL), lambda i: (cond_of_tile(i), 0, 0)),
                  const((D_MODEL, D_MODEL)), const((256, 256)), const((1, 256)), const((1, 256)), const((1, 256)),
                  const((G_CHUNK, 512)), const((G_CHUNK, 256)), const((1, D_MODEL)), const((1, D_MODEL)),
                  const((256, 256))],
        out_specs=row_spec(D_MODEL),
        out_shape=jax.ShapeDtypeStruct((rows, D_MODEL), F32),
        scratch_shapes=[pltpu.VMEM((rows_blk, D_MODEL), BF16), pltpu.VMEM((rows_blk + 2 * SUBLANES, 256), F32)],
        compiler_params=pltpu.CompilerParams(dimension_semantics=("arbitrary",), vmem_limit_bytes=VMEM_LIMIT),
        name="mix_outproj",
    )(x, hf, hb, misc, misc, misc, yf, yb, mod_l, w_out, w_pool_bd, pool_scale, mnorm, snorm, wsp_cat, bsp_exp,
      ln_g, ln_b, mean_sel)


def _ffn_kernel(x_ref, xprev_ref, xnext_ref, mod_ref, wup_ref, cw_ref, cb_ref, wdown_ref, ln_g_ref, ln_b_ref,
                o_ref, h_ref, act_ref, *, per_seq):
    rows = x_ref.shape[0]
    i = pl.program_id(0)
    blk = i % per_seq
    shift = mod_ref[0, 3:4, :]
    scale = mod_ref[0, 4:5, :]
    gate = mod_ref[0, 5:6, :]
    modulate = lambda v: (_ln(v) * (1.0 + scale) + shift).astype(BF16)
    x = x_ref[...]
    h_ref[0:2 * SUBLANES, :] = modulate(jnp.concatenate([xprev_ref[...], xprev_ref[...]], axis=0))
    h_ref[2 * SUBLANES:2 * SUBLANES + rows, :] = modulate(x)
    h_ref[2 * SUBLANES + rows:4 * SUBLANES + rows, :] = modulate(
        jnp.concatenate([xnext_ref[...], xnext_ref[...]], axis=0))
    h = h_ref[...]
    rid = _iota((rows, FF_CHUNK), 0)
    has_prev = (rid > 0) | (blk > 0)
    has_next = (rid < rows - 1) | (blk < per_seq - 1)
    base = 2 * SUBLANES
    n_pad = rows + 4 * SUBLANES

    def conv(u, col):
        up = jnp.where(has_prev, pltpu.roll(u, 1, 0)[base:base + rows, :], 0.0)
        un = jnp.where(has_next, pltpu.roll(u, n_pad - 1, 0)[base:base + rows, :], 0.0)
        return (cw_ref[0:1, col] * up + cw_ref[1:2, col] * u[base:base + rows, :] + cw_ref[2:3, col] * un
                + cb_ref[:, col])

    for c in range(D_FF // FF_CHUNK):
        cg = slice(c * FF_CHUNK, (c + 1) * FF_CHUNK)
        cv = slice(D_FF + c * FF_CHUNK, D_FF + (c + 1) * FF_CHUNK)
        g = conv(_dot(h, wup_ref[:, cg]), cg)
        val = conv(_dot(h, wup_ref[:, cv]), cv)
        act_ref[:, cg] = (_silu(g) * val).astype(BF16)
    ffn = _dot(act_ref[...], wdown_ref[...])
    o_ref[...] = _ln(ALPHA * x + gate * ffn) * ln_g_ref[...] + ln_b_ref[...]


def _ffn_call(x, mod_l, w_up, conv_w, conv_b, w_down, ln_g, ln_b, seq, rows_blk, cond_of_tile):
    rows = x.shape[0]
    nt = rows // rows_blk
    per_seq = seq // rows_blk
    sub_per_blk = rows_blk // SUBLANES
    n_sub = rows // SUBLANES
    row_spec = lambda w: pl.BlockSpec((rows_blk, w), lambda i: (i, 0))
    const = lambda shape: pl.BlockSpec(shape, lambda i: tuple(0 for _ in shape))
    return pl.pallas_call(
        functools.partial(_ffn_kernel, per_seq=per_seq),
        grid=(nt,),
        in_specs=[row_spec(D_MODEL),
                  pl.BlockSpec((SUBLANES, D_MODEL), lambda i: (jnp.maximum(i * sub_per_blk - 1, 0), 0)),
                  pl.BlockSpec((SUBLANES, D_MODEL), lambda i: (jnp.minimum((i + 1) * sub_per_blk, n_sub - 1), 0)),
                  pl.BlockSpec((1, 6, D_MODEL), lambda i: (cond_of_tile(i), 0, 0)),
                  const((D_MODEL, 2 * D_FF)), const((3, 2 * D_FF)), const((1, 2 * D_FF)),
                  const((D_FF, D_MODEL)), const((1, D_MODEL)), const((1, D_MODEL))],
        out_specs=row_spec(D_MODEL),
        out_shape=jax.ShapeDtypeStruct((rows, D_MODEL), F32),
        scratch_shapes=[pltpu.VMEM((rows_blk + 4 * SUBLANES, D_MODEL), BF16), pltpu.VMEM((rows_blk, D_FF), BF16)],
        compiler_params=pltpu.CompilerParams(dimension_semantics=("arbitrary",), vmem_limit_bytes=VMEM_LIMIT),
        name="conv_ffn",
    )(x, x, x, mod_l, w_up, conv_w, conv_b, w_down, ln_g, ln_b)


def _grid_pos_embed(rows, dim):
    quarter = dim // 4
    freq = 1.0 / (POS_BASE ** (jnp.arange(quarter, dtype=F32) / quarter))
    r = jnp.repeat(jnp.arange(rows, dtype=F32), GRID_W)
    col = jnp.tile(jnp.arange(GRID_W, dtype=F32), rows)

    def enc(pos):
        ang = pos[:, None] * freq[None, :]
        return jnp.concatenate([jnp.sin(ang), jnp.cos(ang)], -1)

    return jnp.concatenate([enc(r), enc(col)], -1)


def _expand_selectors():
    m_sel = np.zeros((LANES, 1024), np.float32)
    s_sel = np.zeros((LANES, 512), np.float32)
    for d in range(N_DIR):
        for h in range(HEADS):
            m_sel[d * HEADS + h, d * 512 + h * HD:d * 512 + (h + 1) * HD] = 1.0
            m_sel[8 + d * HEADS + h, d * 512 + 256 + h * HD:d * 512 + 256 + (h + 1) * HD] = 1.0
            s_sel[16 + d * HEADS + h, d * 256 + h * HD:d * 256 + (h + 1) * HD] = 1.0
    mean_sel = np.kron(np.eye(HEADS, dtype=np.float32), np.full((HD, HD), 1.0 / HD, np.float32))
    return jnp.asarray(m_sel, BF16), jnp.asarray(s_sel, BF16), jnp.asarray(mean_sel, BF16)


def _permute_w_in(w):
    pad = jnp.zeros((D_MODEL, LANES - 24), w.dtype)
    return jnp.concatenate([w[:, 0:768], w[:, 768:1024], w[:, 1040:1296], w[:, 1296:1808], w[:, 1808:2064],
                            w[:, 2064:2832], w[:, 1024:1040], w[:, 2832:2840], pad], axis=1).astype(BF16)


def _block_diag4(blocks):
    a, b = blocks.shape[1:]
    eye = jnp.eye(HEADS, dtype=blocks.dtype)
    return (eye[:, None, :, None] * blocks[:, :, None, :]).reshape(HEADS * a, HEADS * b)


def _pack_mlstm_state(c, n, m):
    eye = jnp.eye(HEADS, dtype=F32)
    cbd = (eye[None, None, :, None, :, None] * c[:, :, :, :, None, :]).reshape(c.shape[0], N_DIR, 256, 256)
    nbd = jnp.broadcast_to((eye[None, None, :, None, :, None] * n[:, :, :, :, None, None]),
                           (c.shape[0], N_DIR, HEADS, HD, HEADS, HD)).reshape(c.shape[0], N_DIR, 256, 256)
    mrow = jnp.repeat(m, HD, axis=-1)[:, :, None, :]
    return jnp.concatenate([cbd, nbd], axis=-1), mrow


def _unpack_mlstm_state(caug, mrow):
    b = caug.shape[0]
    blocks = caug.reshape(b, N_DIR, HEADS, HD, 2, HEADS, HD)
    idx = jnp.arange(HEADS)
    diag = blocks[:, :, idx, :, :, idx, :]
    diag = jnp.moveaxis(diag, 0, 2)
    c = diag[:, :, :, :, 0, :]
    n = diag[:, :, :, :, 1, 0]
    m = mrow[:, :, 0, :].reshape(b, N_DIR, HEADS, HD)[..., 0]
    return c, n, m


def _pack_ssd_state(s):
    b = s.shape[0]
    st = jnp.transpose(s, (0, 1, 4, 2, 3))
    grp = (jnp.arange(HEADS) // 2)[None, :] == jnp.arange(2)[:, None]
    full = st[:, :, None, :, :, :] * grp[None, None, :, None, :, None].astype(F32)
    return full.reshape(b, N_DIR, 256, 256)


def _unpack_ssd_state(hp):
    b = hp.shape[0]
    full = hp.reshape(b, N_DIR, 2, S_STATE, HEADS, HD)
    idx = jnp.arange(HEADS)
    sel = full[:, :, idx // 2, :, idx, :]
    return jnp.transpose(sel, (1, 2, 0, 4, 3))


def _layer(x, pos, mod_l, p, m_state, s_state, batch, seq, rows_blk, cond_of_tile, consts):
    m_sel, s_sel, mean_sel = consts
    outs = _inproj_call(x, pos, mod_l, p["w_in"], seq, rows_blk, cond_of_tile)
    if pos is not None:
        qkv, misc, xbc, gates, x = outs
    else:
        qkv, misc, xbc, gates = outs
    hf, hb, caug, mrow = _mlstm_call(qkv, gates, p["m_bias"], m_sel, m_state[0], m_state[1], batch, seq, rows_blk)
    yf, yb, hs = _ssd_call(xbc, gates, p["ssd_conv_w"], p["ssd_conv_b"], p["dt_bias"], s_sel, p["a_log"],
                           p["d_skip"], s_state, batch, seq, rows_blk)
    x1 = _mix_call(x, hf, hb, misc, yf, yb, mod_l, p["w_out"], p["w_pool"], p["pool_scale"], p["mnorm"],
                   p["snorm"], p["wsp"], p["bsp"], p["ln1_g"], p["ln1_b"], mean_sel, seq, rows_blk, cond_of_tile)
    x2 = _ffn_call(x1, mod_l, p["ffn_w_up"], p["ffn_conv_w"], p["ffn_conv_b"], p["ffn_w_down"], p["ln2_g"],
                   p["ln2_b"], seq, rows_blk, cond_of_tile)
    return x2, (caug, mrow), hs


def kernel(x_prompt, x_sample, state_mlstm_c, state_mlstm_n, state_mlstm_m, state_ssd, c, c_ctx, w_ada, b_ada, w_in, b_igate, b_fgate, mlstm_norm_g, w_pool, pool_scale, w_spatial, b_spatial, ssd_conv_w, ssd_conv_b, ssd_dt_bias, ssd_a_log, ssd_d, ssd_norm_g, w_out, ln1_g, ln1_b, ffn_w_up, ffn_conv_w, ffn_conv_b, ffn_w_down, ln2_g, ln2_b):
    n_ctx, t_ctx, _ = x_prompt.shape
    n_lat, t_lat, _ = x_sample.shape
    consts = _expand_selectors()
    pos = _grid_pos_embed(t_lat // GRID_W, D_MODEL)

    cond = jnp.zeros((N_COND, D_MODEL), F32).at[0:n_lat].set(c).at[CTX_COND].set(c_ctx)
    mod = _ada_call(cond, w_ada, b_ada).reshape(DEPTH, N_COND, 6, D_MODEL)

    ctx_blk = min(t_ctx, 256)
    lat_blk = min(t_lat, 512)
    lat_per_seq = t_lat // lat_blk
    ctx_cond = lambda i: CTX_COND
    lat_cond = lambda i: i // lat_per_seq

    y_p = x_prompt.reshape(n_ctx * t_ctx, D_MODEL)
    y_s = x_sample.reshape(n_lat * t_lat, D_MODEL)
    zero_m = (jnp.zeros((n_ctx, N_DIR, 256, 512), F32), jnp.zeros((n_ctx, N_DIR, 1, 256), F32))
    zero_s = jnp.zeros((n_ctx, N_DIR, 256, 256), F32)
    new_c, new_n, new_m, new_s = [], [], [], []
    for l in range(DEPTH):
        m_bias = jnp.zeros((1, LANES), F32).at[0, 0:8].set(b_igate[l].reshape(-1)).at[0, 8:16].set(b_fgate[l].reshape(-1))
        dt_bias = jnp.zeros((1, LANES), F32).at[0, 16:24].set(ssd_dt_bias[l].reshape(-1))
        p = {
            "w_in": _permute_w_in(w_in[l]),
            "m_bias": m_bias,
            "dt_bias": dt_bias,
            "a_log": jnp.repeat(ssd_a_log[l], HD, axis=-1)[:, None, :],
            "d_skip": jnp.repeat(ssd_d[l], HD)[None, :],
            "ssd_conv_w": ssd_conv_w[l],
            "ssd_conv_b": ssd_conv_b[l][None, :],
            "w_out": w_out[l].astype(BF16),
            "w_pool": _block_diag4(w_pool[l]).astype(BF16),
            "pool_scale": pool_scale[l][None, :],
            "mnorm": mlstm_norm_g[l][None, :],
            "snorm": ssd_norm_g[l][None, :],
            "wsp": jnp.transpose(w_spatial[l], (1, 0, 2)).reshape(G_CHUNK, HEADS * G_CHUNK).astype(BF16),
            "bsp": jnp.repeat(b_spatial[l].T, HD, axis=-1),
            "ln1_g": ln1_g[l][None, :], "ln1_b": ln1_b[l][None, :],
            "ffn_w_up": ffn_w_up[l].astype(BF16),
            "ffn_conv_w": ffn_conv_w[l],
            "ffn_conv_b": ffn_conv_b[l][None, :],
            "ffn_w_down": ffn_w_down[l].astype(BF16),
            "ln2_g": ln2_g[l][None, :], "ln2_b": ln2_b[l][None, :],
        }
        y_p, (caug, mrow), hs = _layer(y_p, None, mod[l], p, zero_m, zero_s, n_ctx, t_ctx, ctx_blk, ctx_cond, consts)
        cc, nn, mm = _unpack_mlstm_state(caug, mrow)
        new_c.append(cc)
        new_n.append(nn)
        new_m.append(mm)
        new_s.append(_unpack_ssd_state(hs))
        lat_m = _pack_mlstm_state(state_mlstm_c[:, l], state_mlstm_n[:, l], state_mlstm_m[:, l])
        y_s, _, _ = _layer(y_s, pos if l == 0 else None, mod[l], p, lat_m, _pack_ssd_state(state_ssd[:, l]),
                           n_lat, t_lat, lat_blk, lat_cond, consts)
    return (y_p.reshape(n_ctx, t_ctx, D_MODEL), y_s.reshape(n_lat, t_lat, D_MODEL),
            jnp.stack(new_c, 1), jnp.stack(new_n, 1), jnp.stack(new_m, 1), jnp.stack(new_s, 1))
```

```python
import functools

import numpy as np
import jax
import jax.numpy as jnp
from jax import lax
from jax.experimental import pallas as pl
from jax.experimental.pallas import tpu as pltpu

F32 = jnp.float32
BF16 = jnp.bfloat16

D_MODEL = 1024
DEPTH = 2
GRID_W = 64
POS_BASE = 10000.0
W_GROUP = 256
N_DIR = 2
HEADS = 4
HD = 64
G_CHUNK = 128
S_STATE = 128
S_XBC = 768
D_FF = 2816
FF_CHUNK = 256
ALPHA = (2 * DEPTH) ** 0.25
LN_EPS = 1e-5
N_COND = 16
CTX_COND = 8
LANES = 128
SUBLANES = 8
CHUNK = LANES
HS = HEADS * CHUNK
QKV_W = 768
MISC_W = 1280
IN_W = QKV_W + MISC_W + S_XBC + LANES
M_EXP_W = HS + 4 * W_GROUP
S_EXP_W = HS + 3 * W_GROUP
M_SEL_K = 3 * 48
S_SEL_K = 3 * 32
VMEM_LIMIT = 56 * 1024 * 1024


def _dot(a, b):
    return jnp.dot(a, b, preferred_element_type=F32)


def _dot_nt(a, b):
    return lax.dot_general(a, b, (((1,), (1,)), ((), ())), preferred_element_type=F32)


def _dot_tn(a, b):
    return lax.dot_general(a, b, (((0,), (0,)), ((), ())), preferred_element_type=F32)


def _split3(x):
    hi = x.astype(BF16)
    r1 = x - hi.astype(F32)
    mid = r1.astype(BF16)
    lo = (r1 - mid.astype(F32)).astype(BF16)
    return hi, mid, lo


def _dot_sel2(x, sel):
    hi = x.astype(BF16)
    lo = (x - hi.astype(F32)).astype(BF16)
    return _dot(hi, sel) + _dot(lo, sel)


def _expand(rows_f32, sel):
    hi, mid, lo = _split3(rows_f32)
    return _dot_tn(jnp.concatenate([hi, mid, lo], axis=0), sel)


def _ln(x):
    mu = jnp.mean(x, -1, keepdims=True)
    xc = x - mu
    var = jnp.mean(xc * xc, -1, keepdims=True)
    return xc * lax.rsqrt(var + LN_EPS)


def _sigmoid(x):
    return 1.0 / (1.0 + jnp.exp(-x))


def _silu(x):
    return x * _sigmoid(x)


def _softplus(x):
    return jnp.maximum(x, 0.0) + jnp.log1p(jnp.exp(-jnp.abs(x)))


def _log_sigmoid(x):
    return jnp.minimum(x, 0.0) - jnp.log1p(jnp.exp(-jnp.abs(x)))


def _iota(shape, dim):
    return lax.broadcasted_iota(jnp.int32, shape, dim)


def _lane_scan(x, op, ident, reverse):
    pos = _iota(x.shape, 1)
    k = 1
    while k < LANES:
        if reverse:
            shifted = pltpu.roll(x, LANES - k, 1)
            valid = pos < LANES - k
        else:
            shifted = pltpu.roll(x, k, 1)
            valid = pos >= k
        x = op(x, jnp.where(valid, shifted, ident))
        k *= 2
    return x


def _lane_bcast(x, lane):
    return jnp.broadcast_to(x[:, lane:lane + 1], x.shape)


def _head_rows(x, d):
    return jnp.concatenate([x[d * HEADS + h:d * HEADS + h + 1, :] for h in range(HEADS)], axis=1)


def _tile4(x):
    return jnp.concatenate([x, x, x, x], axis=0)


def _ada_kernel(cond_ref, w_ref, b_ref, o_ref):
    s = _silu(cond_ref[...]).astype(BF16)
    o_ref[0] = _dot(s, w_ref[0].astype(BF16)) + b_ref[0]


def _ada_call(cond, w_ada, b_ada):
    n_out = w_ada.shape[-1]
    tn = 1536
    return pl.pallas_call(
        _ada_kernel,
        grid=(DEPTH, n_out // tn),
        in_specs=[
            pl.BlockSpec((N_COND, D_MODEL), lambda l, n: (0, 0)),
            pl.BlockSpec((1, D_MODEL, tn), lambda l, n: (l, 0, n)),
            pl.BlockSpec((1, 1, tn), lambda l, n: (l, 0, n)),
        ],
        out_specs=pl.BlockSpec((1, N_COND, tn), lambda l, n: (l, 0, n)),
        out_shape=jax.ShapeDtypeStruct((DEPTH, N_COND, n_out), F32),
        compiler_params=pltpu.CompilerParams(
            dimension_semantics=("arbitrary", "arbitrary"), vmem_limit_bytes=VMEM_LIMIT),
        name="ada_mod",
    )(cond, w_ada, b_ada.reshape(DEPTH, 1, n_out))


def _inproj_kernel(*refs, has_pos):
    if has_pos:
        x_ref, pos_ref, mod_ref, w_ref, qkv_ref, misc_ref, xbc_ref, gates_ref, xres_ref = refs
        x = x_ref[...] + pos_ref[...]
        xres_ref[...] = x
    else:
        x_ref, mod_ref, w_ref, qkv_ref, misc_ref, xbc_ref, gates_ref = refs
        x = x_ref[...]
    shift = mod_ref[0, 0:1, :]
    scale = mod_ref[0, 1:2, :]
    h = (_ln(x) * (1.0 + scale) + shift).astype(BF16)
    qkv = _dot(h, w_ref[:, 0:QKV_W])
    qkv_ref[:, 0:256] = qkv[:, 0:256].astype(BF16)
    qkv_ref[:, 256:512] = (qkv[:, 256:512] * (HD ** -0.5)).astype(BF16)
    qkv_ref[:, 512:768] = qkv[:, 512:768].astype(BF16)
    misc_ref[...] = _dot(h, w_ref[:, QKV_W:QKV_W + MISC_W])
    xbc_ref[...] = _dot(h, w_ref[:, QKV_W + MISC_W:QKV_W + MISC_W + S_XBC])
    gates_ref[...] = _dot(h, w_ref[:, QKV_W + MISC_W + S_XBC:IN_W])


def _inproj_call(x, pos, mod_l, w_in_p, seq, rows_blk, cond_of_tile):
    rows = x.shape[0]
    nt = rows // rows_blk
    per_seq = seq // rows_blk
    has_pos = pos is not None
    row_spec = lambda w: pl.BlockSpec((rows_blk, w), lambda i: (i, 0))
    in_specs = [row_spec(D_MODEL)]
    args = [x]
    if has_pos:
        in_specs.append(pl.BlockSpec((rows_blk, D_MODEL), lambda i: (i % per_seq, 0)))
        args.append(pos)
    in_specs += [
        pl.BlockSpec((1, 6, D_MODEL), lambda i: (cond_of_tile(i), 0, 0)),
        pl.BlockSpec((D_MODEL, IN_W), lambda i: (0, 0)),
    ]
    args += [mod_l, w_in_p]
    out_specs = [row_spec(QKV_W), row_spec(MISC_W), row_spec(S_XBC), row_spec(LANES)]
    out_shape = [jax.ShapeDtypeStruct((rows, QKV_W), BF16), jax.ShapeDtypeStruct((rows, MISC_W), F32),
                 jax.ShapeDtypeStruct((rows, S_XBC), F32), jax.ShapeDtypeStruct((rows, LANES), F32)]
    if has_pos:
        out_specs.append(row_spec(D_MODEL))
        out_shape.append(jax.ShapeDtypeStruct((rows, D_MODEL), F32))
    return pl.pallas_call(
        functools.partial(_inproj_kernel, has_pos=has_pos),
        grid=(nt,), in_specs=in_specs, out_specs=out_specs, out_shape=out_shape,
        compiler_params=pltpu.CompilerParams(dimension_semantics=("arbitrary",), vmem_limit_bytes=VMEM_LIMIT),
        name="inproj",
    )(*args)


def _mlstm_prepare(gates, bias_row, sel, m_prev, d, nb, exp_ref, grow_ref):
    reverse = bool(d)
    last = 0 if reverse else LANES - 1
    gt = (gates + bias_row).T
    parts = [[None] * nb for _ in range(5)]
    for c in (range(nb - 1, -1, -1) if reverse else range(nb)):
        sl = slice(c * CHUNK, (c + 1) * CHUNK)
        bcs = _lane_scan(_log_sigmoid(gt[8:16, sl]), jnp.add, 0.0, reverse)
        g = gt[0:8, sl] - bcs
        gmax = _lane_scan(g, jnp.maximum, -jnp.inf, reverse)
        m_last = jnp.maximum(m_prev, _lane_bcast(gmax, last))
        m_q = jnp.maximum(m_prev, gmax)
        parts[0][c] = -m_q
        parts[1][c] = jnp.exp(m_prev - m_q)
        parts[2][c] = -(bcs + m_q)
        parts[3][c] = jnp.exp(g - m_last)
        parts[4][c] = jnp.exp(m_prev - m_last)
        grow_ref[d, c:c + 1, :] = _head_rows(g, d)
        m_prev = _lane_bcast(bcs, last) + m_last
    rows = [jnp.concatenate(p, axis=1) for p in parts]
    rows.append(jnp.zeros_like(rows[0]))
    exp_ref[d] = _expand(jnp.concatenate(rows, axis=0), sel)
    return m_prev


def _mlstm_kernel(qkvf_ref, qkvb_ref, gf_ref, gb_ref, bias_ref, sel_ref, c0_ref, m0_ref,
                  hf_ref, hb_ref, cout_ref, mout_ref, caug_ref, m_ref, exp_ref, grow_ref, *, nb):
    j = pl.program_id(1)
    nblk = pl.num_programs(1)

    @pl.when(j == 0)
    def _():
        caug_ref[...] = c0_ref[0]
        m_ref[...] = m0_ref[0]

    lane = _iota((CHUNK, HS), 1)
    row = _iota((CHUNK, HS), 0)
    causal = ((lane % CHUNK) <= row, (lane % CHUNK) >= row)
    bd_hs = (_iota((HS, 256), 0) // CHUNK) == (_iota((HS, 256), 1) // HD)
    bd = (_iota((256, 256), 0) // HD) == (_iota((256, 256), 1) // HD)
    bd2 = jnp.concatenate([bd, bd], axis=1)
    ones_bd = jnp.where(bd_hs, 1.0, 0.0).astype(BF16)
    ones_rows = jnp.ones((CHUNK, 256), BF16)
    zero = jnp.zeros((), BF16)

    qkv_refs = (qkvf_ref, qkvb_ref)
    g_refs = (gf_ref, gb_ref)
    h_refs = (hf_ref, hb_ref)
    for d in range(N_DIR):
        m_ref[d] = _mlstm_prepare(g_refs[d][...], bias_ref[...], sel_ref[d], m_ref[d], d, nb, exp_ref, grow_ref)

    for i in range(nb):
        for d in range(N_DIR):
            c = (nb - 1 - i) if d else i
            rows = slice(c * CHUNK, (c + 1) * CHUNK)
            q = qkv_refs[d][rows, 0:256]
            k = qkv_refs[d][rows, 256:512]
            v = qkv_refs[d][rows, 512:768]
            s = _dot_nt(q, jnp.where(bd_hs, _tile4(k), zero))
            expo = exp_ref[d, rows, 0:HS] + grow_ref[d, c:c + 1, :]
            p = (s * jnp.exp(jnp.where(causal[d], expo, -jnp.inf))).astype(BF16)
            vaug = jnp.concatenate([jnp.where(bd_hs, _tile4(v), zero), ones_bd], axis=1)
            nd = _dot(p, vaug)
            qc = _dot(q, caug_ref[d].astype(BF16))
            ew = exp_ref[d, rows, HS:HS + 256]
            num = nd[:, 0:256] + ew * qc[:, 0:256]
            den = nd[:, 256:512] + ew * qc[:, 256:512]
            h_refs[d][rows, :] = num / jnp.maximum(jnp.abs(den), jnp.exp(exp_ref[d, rows, HS + 256:HS + 512]))
            kw = (k.astype(F32) * exp_ref[d, rows, HS + 512:HS + 768]).astype(BF16)
            u = _dot_tn(kw, jnp.concatenate([v, ones_rows], axis=1))
            wc = exp_ref[d, c * CHUNK:c * CHUNK + 1, HS + 768:HS + 1024]
            caug_ref[d] = jnp.concatenate([wc, wc], axis=1) * caug_ref[d] + jnp.where(bd2, u, 0.0)

    @pl.when(j == nblk - 1)
    def _():
        mout_ref[0] = m_ref[...]
        for d in range(N_DIR):
            for h in range(HEADS):
                r = slice(h * HD, (h + 1) * HD)
                cout_ref[0, d, r, 0:HD] = caug_ref[d, r, h * HD:(h + 1) * HD]
                cout_ref[0, d, r, HD:2 * HD] = caug_ref[d, r, 256 + h * HD:256 + (h + 1) * HD]


def _mlstm_call(qkv, gates, bias_row, sel, c0, m0, batch, seq, rows_blk):
    rows = qkv.shape[0]
    nblk = seq // rows_blk
    nb = rows_blk // CHUNK
    fwd = lambda w: pl.BlockSpec((rows_blk, w), lambda b, j: (b * nblk + j, 0))
    bwd = lambda w: pl.BlockSpec((rows_blk, w), lambda b, j: (b * nblk + nblk - 1 - j, 0))
    per_b = lambda shape: pl.BlockSpec((1,) + shape, lambda b, j: (b,) + tuple(0 for _ in shape))
    return pl.pallas_call(
        functools.partial(_mlstm_kernel, nb=nb),
        grid=(batch, nblk),
        in_specs=[fwd(QKV_W), bwd(QKV_W), fwd(LANES), bwd(LANES),
                  pl.BlockSpec((1, LANES), lambda b, j: (0, 0)),
                  pl.BlockSpec((N_DIR, M_SEL_K, M_EXP_W), lambda b, j: (0, 0, 0)),
                  per_b((N_DIR, 256, 512)), per_b((N_DIR, SUBLANES, LANES))],
        out_specs=[fwd(256), bwd(256), per_b((N_DIR, 256, LANES)), per_b((N_DIR, SUBLANES, LANES))],
        out_shape=[jax.ShapeDtypeStruct((rows, 256), F32), jax.ShapeDtypeStruct((rows, 256), F32),
                   jax.ShapeDtypeStruct((batch, N_DIR, 256, LANES), F32),
                   jax.ShapeDtypeStruct((batch, N_DIR, SUBLANES, LANES), F32)],
        scratch_shapes=[pltpu.VMEM((N_DIR, 256, 512), F32), pltpu.VMEM((N_DIR, SUBLANES, LANES), F32),
                        pltpu.VMEM((N_DIR, rows_blk, M_EXP_W), F32), pltpu.VMEM((N_DIR, SUBLANES, HS), F32)],
        compiler_params=pltpu.CompilerParams(
            dimension_semantics=("arbitrary", "arbitrary"), vmem_limit_bytes=VMEM_LIMIT),
        name="mlstm",
    )(qkv, qkv, gates, gates, bias_row, sel, c0, m0)


def _ssd_prepare(xbc_ref, prev_ref, next_ref, first, last_blk, gates, cw_ref, cb_ref, dtb_ref, alog_ref, sel,
                 d, nb, xs_ref, bc_ref, exp_ref, grow_ref, dtrow_ref):
    reverse = bool(d)
    last = 0 if reverse else LANES - 1
    x = xbc_ref[...]
    n_rows = x.shape[0]
    rid = _iota(x.shape, 0)
    xp = jnp.where(first, 0.0, prev_ref[SUBLANES - 1:SUBLANES, :])
    xn = jnp.where(last_blk, 0.0, next_ref[0:1, :])
    x_prev = jnp.where(rid == 0, xp, pltpu.roll(x, 1, 0))
    x_next = jnp.where(rid == n_rows - 1, xn, pltpu.roll(x, n_rows - 1, 0))
    act = _silu(cw_ref[0:1, :] * x_prev + cw_ref[1:2, :] * x + cw_ref[2:3, :] * x_next + cb_ref[...])
    xs_ref[d] = act[:, 0:256]
    bc_ref[d] = act[:, 256:768].astype(BF16)
    gt = (gates + dtb_ref[...]).T
    a = -jnp.exp(alog_ref[...])
    parts = [[None] * nb for _ in range(4)]
    for c in range(nb):
        dt = _softplus(gt[16:24, c * CHUNK:(c + 1) * CHUNK])
        lc = _lane_scan(dt * a, jnp.add, 0.0, reverse)
        lc_last = _lane_bcast(lc, last)
        parts[0][c] = lc
        parts[1][c] = jnp.exp(lc)
        parts[2][c] = jnp.exp(lc_last - lc) * dt
        parts[3][c] = jnp.exp(lc_last)
        grow_ref[d, c:c + 1, :] = _head_rows(-lc, d)
        dtrow_ref[d, c:c + 1, :] = _head_rows(dt, d)
    exp_ref[d] = _expand(jnp.concatenate([jnp.concatenate(p, axis=1) for p in parts], axis=0), sel)


def _ssd_kernel(xf_ref, xfp_ref, xfn_ref, xb_ref, xbp_ref, xbn_ref, gf_ref, gb_ref,
                cw_ref, cb_ref, dtb_ref, sel_ref, alog_ref, dskip_ref, h0_ref,
                yf_ref, yb_ref, hout_ref,
                h_ref, xs_ref, bc_ref, exp_ref, grow_ref, dtrow_ref, *, nb):
    j = pl.program_id(1)
    nblk = pl.num_programs(1)

    @pl.when(j == 0)
    def _():
        h_ref[...] = h0_ref[0]

    lane = _iota((CHUNK, HS), 1)
    row = _iota((CHUNK, HS), 0)
    causal = ((lane % CHUNK) <= row, (lane % CHUNK) >= row)
    r_hs = _iota((HS, 256), 0)
    c_hs = _iota((HS, 256), 1)
    bd_hs = (r_hs // CHUNK) == (c_hs // HD)
    gmask_b = (r_hs // CHUNK // 2) == (c_hs // S_STATE)
    gmask_h = (_iota((256, 256), 0) // S_STATE) == (_iota((256, 256), 1) // HD // 2)
    zero = jnp.zeros((), BF16)

    x_refs = ((xf_ref, xfp_ref, xfn_ref), (xb_ref, xbp_ref, xbn_ref))
    g_refs = (gf_ref, gb_ref)
    y_refs = (yf_ref, yb_ref)
    for d in range(N_DIR):
        blk = (nblk - 1 - j) if d else j
        _ssd_prepare(x_refs[d][0], x_refs[d][1], x_refs[d][2], blk == 0, blk == nblk - 1, g_refs[d][...],
                     cw_ref, cb_ref, dtb_ref, alog_ref, sel_ref[d], d, nb, xs_ref, bc_ref, exp_ref, grow_ref,
                     dtrow_ref)

    for i in range(nb):
        for d in range(N_DIR):
            c = (nb - 1 - i) if d else i
            rows = slice(c * CHUNK, (c + 1) * CHUNK)
            xs = xs_ref[d, rows, :]
            bm = bc_ref[d, rows, 0:256]
            cm = bc_ref[d, rows, 256:512]
            cb = _dot_nt(cm, jnp.where(gmask_b, _tile4(bm), zero))
            expo = exp_ref[d, rows, 0:HS] + grow_ref[d, c:c + 1, :]
            dmat = jnp.exp(jnp.where(causal[d], expo, -jnp.inf)) * dtrow_ref[d, c:c + 1, :]
            p = (cb * dmat).astype(BF16)
            xbd = jnp.where(bd_hs, _tile4(xs.astype(BF16)), zero)
            y = _dot(p, xbd) + exp_ref[d, rows, HS:HS + 256] * _dot(cm, h_ref[d].astype(BF16))
            if d == 0:
                y = y + dskip_ref[...] * xs
            y_refs[d][rows, :] = y
            xw = (xs * exp_ref[d, rows, HS + 256:HS + 512]).astype(BF16)
            u = _dot_tn(bm, xw)
            da = exp_ref[d, c * CHUNK:c * CHUNK + 1, HS + 512:HS + 768]
            h_ref[d] = da * h_ref[d] + jnp.where(gmask_h, u, 0.0)

    @pl.when(j == nblk - 1)
    def _():
        hout_ref[0] = h_ref[...]


def _ssd_call(xbc, gates, conv_w, conv_b, dtb_row, sel, alog8, dskip_exp, h0, batch, seq, rows_blk):
    rows = xbc.shape[0]
    nblk = seq // rows_blk
    nb = rows_blk // CHUNK
    sub_per_blk = rows_blk // SUBLANES
    n_sub = rows // SUBLANES
    fblk = lambda b, j: b * nblk + j
    bblk = lambda b, j: b * nblk + nblk - 1 - j
    main = lambda f, w: pl.BlockSpec((rows_blk, w), lambda b, j: (f(b, j), 0))
    prev = lambda f: pl.BlockSpec((SUBLANES, S_XBC), lambda b, j: (jnp.maximum(f(b, j) * sub_per_blk - 1, 0), 0))
    nxt = lambda f: pl.BlockSpec((SUBLANES, S_XBC),
                                 lambda b, j: (jnp.minimum((f(b, j) + 1) * sub_per_blk, n_sub - 1), 0))
    const = lambda shape: pl.BlockSpec(shape, lambda b, j: tuple(0 for _ in shape))
    return pl.pallas_call(
        functools.partial(_ssd_kernel, nb=nb),
        grid=(batch, nblk),
        in_specs=[main(fblk, S_XBC), prev(fblk), nxt(fblk), main(bblk, S_XBC), prev(bblk), nxt(bblk),
                  main(fblk, LANES), main(bblk, LANES),
                  const((3, S_XBC)), const((1, S_XBC)), const((1, LANES)), const((N_DIR, S_SEL_K, S_EXP_W)),
                  const((SUBLANES, LANES)), const((1, 256)),
                  pl.BlockSpec((1, N_DIR, 256, 256), lambda b, j: (b, 0, 0, 0))],
        out_specs=[main(fblk, 256), main(bblk, 256),
                   pl.BlockSpec((1, N_DIR, 256, 256), lambda b, j: (b, 0, 0, 0))],
        out_shape=[jax.ShapeDtypeStruct((rows, 256), F32), jax.ShapeDtypeStruct((rows, 256), F32),
                   jax.ShapeDtypeStruct((batch, N_DIR, 256, 256), F32)],
        scratch_shapes=[pltpu.VMEM((N_DIR, 256, 256), F32), pltpu.VMEM((N_DIR, rows_blk, 256), F32),
                        pltpu.VMEM((N_DIR, rows_blk, 512), BF16), pltpu.VMEM((N_DIR, rows_blk, S_EXP_W), F32),
                        pltpu.VMEM((N_DIR, SUBLANES, HS), F32), pltpu.VMEM((N_DIR, SUBLANES, HS), F32)],
        compiler_params=pltpu.CompilerParams(
            dimension_semantics=("arbitrary", "arbitrary"), vmem_limit_bytes=VMEM_LIMIT),
        name="ssd",
    )(xbc, xbc, xbc, xbc, xbc, xbc, gates, gates, conv_w, conv_b, dtb_row, sel, alog8, dskip_exp, h0)


def _mix_kernel(x_ref, hf_ref, hb_ref, misc_ref, pprev_ref, pnext_ref, yf_ref, yb_ref, mod_ref,
                wout_ref, wpool_ref, pscale_ref, mnorm_ref, snorm_ref, wsp_ref, bsp_ref, ln_g_ref, ln_b_ref,
                mean_sel_ref, o_ref, cat_ref, pad_ref, *, seq, per_seq):
    rows = x_ref.shape[0]
    i = pl.program_id(0)
    hs = hf_ref[...] + hb_ref[...]
    mu = _dot_sel2(hs, mean_sel_ref[...])
    hc = hs - mu
    var = _dot_sel2(hc * hc, mean_sel_ref[...])
    y_m = _sigmoid(misc_ref[:, 0:256]) * (hc * lax.rsqrt(var + LN_EPS) * mnorm_ref[...])
    cat_ref[:, 0:256] = y_m.astype(BF16)
    blk = i % per_seq
    xp = misc_ref[:, 256:512]
    pad_ref[0:SUBLANES, :] = jnp.where(blk == 0, 0.0, pprev_ref[...])
    pad_ref[SUBLANES:SUBLANES + rows, :] = xp
    pad_ref[SUBLANES + rows:2 * SUBLANES + rows, :] = jnp.where(blk == per_seq - 1, 0.0, pnext_ref[...])
    n_pad = rows + 2 * SUBLANES
    back = lambda a, k: pltpu.roll(a, k, 0)
    ahead = lambda a, k: pltpu.roll(a, n_pad - k, 0)
    s2 = pad_ref[...]
    s2 = s2 + back(s2, 1)
    s4 = s2 + back(s2, 2)
    s8 = s4 + back(s4, 4)
    s16 = s8 + back(s8, 8)
    grp = _iota((n_pad, 256), 1) // HD
    wsum = jnp.where(grp == 0, s2, jnp.where(grp == 1, ahead(s4, 1), jnp.where(grp == 2, ahead(s8, 3), ahead(s16, 7))))
    wsum = wsum[SUBLANES:SUBLANES + rows, :]
    win = jnp.left_shift(2, _iota((rows, 256), 1) // HD)
    half = win // 2
    t = blk * rows + _iota((rows, 256), 0)
    cnt = jnp.clip(t - half + win, 0, seq) - jnp.clip(t - half, 0, seq)
    pooled = wsum / cnt.astype(F32) - xp
    y_p = _dot(pooled.astype(BF16), wpool_ref[...]) * pscale_ref[...]
    cat_ref[:, 256:512] = y_p.astype(BF16)
    vn = _ln(misc_ref[:, 768:1024]).astype(BF16)
    bd = (_iota((512, 256), 0) // G_CHUNK) == (_iota((512, 256), 1) // HD)
    for c in range(rows // G_CHUNK):
        r = slice(c * G_CHUNK, (c + 1) * G_CHUNK)
        vbd = jnp.where(bd, _tile4(vn[r, :]), jnp.zeros((), BF16))
        mixed = _dot(wsp_ref[...], vbd) + bsp_ref[...]
        cat_ref[r, 512:768] = (misc_ref[r, 512:768] * mixed).astype(BF16)
    y = (yf_ref[...] + yb_ref[...]) * _silu(misc_ref[:, 1024:1280])
    for g in range(2):
        yg = y[:, g * LANES:(g + 1) * LANES]
        yg = yg * lax.rsqrt(jnp.mean(yg * yg, -1, keepdims=True) + LN_EPS)
        cat_ref[:, 768 + g * LANES:768 + (g + 1) * LANES] = (yg * snorm_ref[:, g * LANES:(g + 1) * LANES]).astype(BF16)
    mix = _dot(cat_ref[...], wout_ref[...])
    gate = mod_ref[0, 2:3, :]
    o_ref[...] = _ln(ALPHA * x_ref[...] + gate * mix) * ln_g_ref[...] + ln_b_ref[...]


def _mix_call(x, hf, hb, misc, yf, yb, mod_l, w_out, w_pool_bd, pool_scale, mnorm, snorm, wsp_cat, bsp_exp,
              ln_g, ln_b, mean_sel, seq, rows_blk, cond_of_tile):
    rows = x.shape[0]
    nt = rows // rows_blk
    per_seq = seq // rows_blk
    sub_per_blk = rows_blk // SUBLANES
    n_sub = rows // SUBLANES
    row_spec = lambda w: pl.BlockSpec((rows_blk, w), lambda i: (i, 0))
    const = lambda shape: pl.BlockSpec(shape, lambda i: tuple(0 for _ in shape))
    return pl.pallas_call(
        functools.partial(_mix_kernel, seq=seq, per_seq=per_seq),
        grid=(nt,),
        in_specs=[row_spec(D_MODEL), row_spec(256), row_spec(256), row_spec(MISC_W),
                  pl.BlockSpec((SUBLANES, 256), lambda i: (jnp.maximum(i * sub_per_blk - 1, 0), 1)),
                  pl.BlockSpec((SUBLANES, 256), lambda i: (jnp.minimum((i + 1) * sub_per_blk, n_sub - 1), 1)),
                  row_spec(256), row_spec(256),
                  pl.BlockSpec((1, 6, D_MODEL), lambda i: (cond_of_tile(i), 0, 0)),
                  const((D_MODEL, D_MODEL)), const((256, 256)), const((1, 256)), const((1, 256)), const((1, 256)),
                  const((G_CHUNK, 512)), const((G_CHUNK, 256)), const((1, D_MODEL)), const((1, D_MODEL)),
                  const((256, 256))],
        out_specs=row_spec(D_MODEL),
        out_shape=jax.ShapeDtypeStruct((rows, D_MODEL), F32),
        scratch_shapes=[pltpu.VMEM((rows_blk, D_MODEL), BF16), pltpu.VMEM((rows_blk + 2 * SUBLANES, 256), F32)],
        compiler_params=pltpu.CompilerParams(dimension_semantics=("arbitrary",), vmem_limit_bytes=VMEM_LIMIT),
        name="mix_outproj",
    )(x, hf, hb, misc, misc, misc, yf, yb, mod_l, w_out, w_pool_bd, pool_scale, mnorm, snorm, wsp_cat, bsp_exp,
      ln_g, ln_b, mean_sel)


def _ffn_kernel(x_ref, xprev_ref, xnext_ref, mod_ref, wup_ref, cw_ref, cb_ref, wdown_ref, ln_g_ref, ln_b_ref,
                o_ref, h_ref, act_ref, *, per_seq):
    rows = x_ref.shape[0]
    i = pl.program_id(0)
    blk = i % per_seq
    shift = mod_ref[0, 3:4, :]
    scale = mod_ref[0, 4:5, :]
    gate = mod_ref[0, 5:6, :]
    modulate = lambda v: (_ln(v) * (1.0 + scale) + shift).astype(BF16)
    x = x_ref[...]
    base = 2 * SUBLANES
    n_pad = rows + 2 * base
    hp = modulate(jnp.concatenate([xprev_ref[...], xprev_ref[...]], axis=0))
    hn = modulate(jnp.concatenate([xnext_ref[...], xnext_ref[...]], axis=0))
    h_ref[0:base, :] = jnp.where(blk > 0, hp, jnp.zeros((), BF16))
    h_ref[base:base + rows, :] = modulate(x)
    h_ref[base + rows:n_pad, :] = jnp.where(blk < per_seq - 1, hn, jnp.zeros((), BF16))
    h = h_ref[...]

    def conv(u, col):
        up = pltpu.roll(u, 1, 0)[base:base + rows, :]
        un = pltpu.roll(u, n_pad - 1, 0)[base:base + rows, :]
        return (cw_ref[0:1, col] * up + cw_ref[1:2, col] * u[base:base + rows, :] + cw_ref[2:3, col] * un
                + cb_ref[:, col])

    for c in range(D_FF // FF_CHUNK):
        cg = slice(c * FF_CHUNK, (c + 1) * FF_CHUNK)
        cv = slice(D_FF + c * FF_CHUNK, D_FF + (c + 1) * FF_CHUNK)
        g = conv(_dot(h, wup_ref[:, cg]), cg)
        val = conv(_dot(h, wup_ref[:, cv]), cv)
        act_ref[:, cg] = (_silu(g) * val).astype(BF16)
    ffn = _dot(act_ref[...], wdown_ref[...])
    o_ref[...] = _ln(ALPHA * x + gate * ffn) * ln_g_ref[...] + ln_b_ref[...]


def _ffn_call(x, mod_l, w_up, conv_w, conv_b, w_down, ln_g, ln_b, seq, rows_blk, cond_of_tile):
    rows = x.shape[0]
    nt = rows // rows_blk
    per_seq = seq // rows_blk
    sub_per_blk = rows_blk // SUBLANES
    n_sub = rows // SUBLANES
    row_spec = lambda w: pl.BlockSpec((rows_blk, w), lambda i: (i, 0))
    const = lambda shape: pl.BlockSpec(shape, lambda i: tuple(0 for _ in shape))
    return pl.pallas_call(
        functools.partial(_ffn_kernel, per_seq=per_seq),
        grid=(nt,),
        in_specs=[row_spec(D_MODEL),
                  pl.BlockSpec((SUBLANES, D_MODEL), lambda i: (jnp.maximum(i * sub_per_blk - 1, 0), 0)),
                  pl.BlockSpec((SUBLANES, D_MODEL), lambda i: (jnp.minimum((i + 1) * sub_per_blk, n_sub - 1), 0)),
                  pl.BlockSpec((1, 6, D_MODEL), lambda i: (cond_of_tile(i), 0, 0)),
                  const((D_MODEL, 2 * D_FF)), const((3, 2 * D_FF)), const((1, 2 * D_FF)),
                  const((D_FF, D_MODEL)), const((1, D_MODEL)), const((1, D_MODEL))],
        out_specs=row_spec(D_MODEL),
        out_shape=jax.ShapeDtypeStruct((rows, D_MODEL), F32),
        scratch_shapes=[pltpu.VMEM((rows_blk + 4 * SUBLANES, D_MODEL), BF16), pltpu.VMEM((rows_blk, D_FF), BF16)],
        compiler_params=pltpu.CompilerParams(dimension_semantics=("arbitrary",), vmem_limit_bytes=VMEM_LIMIT),
        name="conv_ffn",
    )(x, x, x, mod_l, w_up, conv_w, conv_b, w_down, ln_g, ln_b)


def _grid_pos_embed(rows, dim):
    quarter = dim // 4
    freq = 1.0 / (POS_BASE ** (jnp.arange(quarter, dtype=F32) / quarter))
    r = jnp.repeat(jnp.arange(rows, dtype=F32), GRID_W)
    col = jnp.tile(jnp.arange(GRID_W, dtype=F32), rows)

    def enc(pos):
        ang = pos[:, None] * freq[None, :]
        return jnp.concatenate([jnp.sin(ang), jnp.cos(ang)], -1)

    return jnp.concatenate([enc(r), enc(col)], -1)


def _selector(n_quant, pad_rows, widths):
    sel = np.zeros((N_DIR, 3 * pad_rows, sum(widths)), np.float32)
    offs = np.concatenate([[0], np.cumsum(widths)])
    for d in range(N_DIR):
        for p in range(3):
            for q in range(n_quant):
                per_head = widths[q] // HEADS
                for h in range(HEADS):
                    lo = offs[q] + h * per_head
                    sel[d, p * pad_rows + q * SUBLANES + d * HEADS + h, lo:lo + per_head] = 1.0
    return jnp.asarray(sel, BF16)


def _constants():
    m_sel = _selector(5, 48, [HS, 256, 256, 256, 256])
    s_sel = _selector(4, 32, [HS, 256, 256, 256])
    mean_sel = np.kron(np.eye(HEADS, dtype=np.float32), np.full((HD, HD), 1.0 / HD, np.float32))
    return m_sel, s_sel, jnp.asarray(mean_sel, BF16)


def _permute_w_in(w):
    pad = jnp.zeros((D_MODEL, LANES - 24), w.dtype)
    return jnp.concatenate([w[:, 0:768], w[:, 768:1024], w[:, 1040:1296], w[:, 1296:1808], w[:, 1808:2064],
                            w[:, 2064:2832], w[:, 1024:1040], w[:, 2832:2840], pad], axis=1).astype(BF16)


def _block_diag4(blocks):
    a, b = blocks.shape[1:]
    eye = jnp.eye(HEADS, dtype=blocks.dtype)
    return (eye[:, None, :, None] * blocks[:, :, None, :]).reshape(HEADS * a, HEADS * b)


def _dir_head_rows(v):
    rep = jnp.broadcast_to(v[..., None], v.shape + (LANES,))
    zeros = jnp.zeros_like(rep[..., 0, :, :])
    return jnp.stack([jnp.concatenate([rep[..., 0, :, :], zeros], axis=-2),
                      jnp.concatenate([zeros, rep[..., 1, :, :]], axis=-2)], axis=-3)


def _pack_mlstm_state(c, n, m):
    eye = jnp.eye(HEADS, dtype=F32)
    cbd = (eye[None, None, :, None, :, None] * c[:, :, :, :, None, :]).reshape(c.shape[0], N_DIR, 256, 256)
    nbd = jnp.broadcast_to((eye[None, None, :, None, :, None] * n[:, :, :, :, None, None]),
                           (c.shape[0], N_DIR, HEADS, HD, HEADS, HD)).reshape(c.shape[0], N_DIR, 256, 256)
    return jnp.concatenate([cbd, nbd], axis=-1), _dir_head_rows(m)


def _unpack_mlstm_state(cout, mout):
    b = cout.shape[0]
    c = cout[..., 0:HD].reshape(b, N_DIR, HEADS, HD, HD)
    n = cout[..., HD].reshape(b, N_DIR, HEADS, HD)
    m = jnp.stack([mout[:, d, d * HEADS:(d + 1) * HEADS, 0] for d in range(N_DIR)], axis=1)
    return c, n, m


def _pack_ssd_state(s):
    b = s.shape[0]
    st = jnp.transpose(s, (0, 1, 4, 2, 3))
    grp = (jnp.arange(HEADS) // 2)[None, :] == jnp.arange(2)[:, None]
    full = st[:, :, None, :, :, :] * grp[None, None, :, None, :, None].astype(F32)
    return full.reshape(b, N_DIR, 256, 256)


def _unpack_ssd_state(hp):
    heads = [jnp.swapaxes(hp[:, :, (h // 2) * S_STATE:(h // 2 + 1) * S_STATE, h * HD:(h + 1) * HD], -1, -2)
             for h in range(HEADS)]
    return jnp.stack(heads, axis=2)


def _layer(x, pos, mod_l, p, m_state, s_state, batch, seq, rows_blk, cond_of_tile, consts):
    m_sel, s_sel, mean_sel = consts
    outs = _inproj_call(x, pos, mod_l, p["w_in"], seq, rows_blk, cond_of_tile)
    if pos is not None:
        qkv, misc, xbc, gates, x = outs
    else:
        qkv, misc, xbc, gates = outs
    hf, hb, cout, mout = _mlstm_call(qkv, gates, p["m_bias"], m_sel, m_state[0], m_state[1], batch, seq, rows_blk)
    yf, yb, hs = _ssd_call(xbc, gates, p["ssd_conv_w"], p["ssd_conv_b"], p["dt_bias"], s_sel, p["a_log"],
                           p["d_skip"], s_state, batch, seq, rows_blk)
    x1 = _mix_call(x, hf, hb, misc, yf, yb, mod_l, p["w_out"], p["w_pool"], p["pool_scale"], p["mnorm"],
                   p["snorm"], p["wsp"], p["bsp"], p["ln1_g"], p["ln1_b"], mean_sel, seq, rows_blk, cond_of_tile)
    x2 = _ffn_call(x1, mod_l, p["ffn_w_up"], p["ffn_conv_w"], p["ffn_conv_b"], p["ffn_w_down"], p["ln2_g"],
                   p["ln2_b"], seq, rows_blk, cond_of_tile)
    return x2, (cout, mout), hs


def kernel(x_prompt, x_sample, state_mlstm_c, state_mlstm_n, state_mlstm_m, state_ssd, c, c_ctx, w_ada, b_ada, w_in, b_igate, b_fgate, mlstm_norm_g, w_pool, pool_scale, w_spatial, b_spatial, ssd_conv_w, ssd_conv_b, ssd_dt_bias, ssd_a_log, ssd_d, ssd_norm_g, w_out, ln1_g, ln1_b, ffn_w_up, ffn_conv_w, ffn_conv_b, ffn_w_down, ln2_g, ln2_b):
    n_ctx, t_ctx, _ = x_prompt.shape
    n_lat, t_lat, _ = x_sample.shape
    consts = _constants()
    pos = _grid_pos_embed(t_lat // GRID_W, D_MODEL)

    cond = jnp.zeros((N_COND, D_MODEL), F32).at[0:n_lat].set(c).at[CTX_COND].set(c_ctx)
    mod = _ada_call(cond, w_ada, b_ada).reshape(DEPTH, N_COND, 6, D_MODEL)

    ctx_blk = min(t_ctx, 256)
    lat_blk = min(t_lat, 512)
    lat_per_seq = t_lat // lat_blk
    ctx_cond = lambda i: CTX_COND
    lat_cond = lambda i: i // lat_per_seq

    y_p = x_prompt.reshape(n_ctx * t_ctx, D_MODEL)
    y_s = x_sample.reshape(n_lat * t_lat, D_MODEL)
    zero_m = (jnp.zeros((n_ctx, N_DIR, 256, 512), F32), jnp.zeros((n_ctx, N_DIR, SUBLANES, LANES), F32))
    zero_s = jnp.zeros((n_ctx, N_DIR, 256, 256), F32)
    new_c, new_n, new_m, new_s = [], [], [], []
    for l in range(DEPTH):
        m_bias = jnp.zeros((1, LANES), F32).at[0, 0:8].set(b_igate[l].reshape(-1)).at[0, 8:16].set(b_fgate[l].reshape(-1))
        dt_bias = jnp.zeros((1, LANES), F32).at[0, 16:24].set(ssd_dt_bias[l].reshape(-1))
        p = {
            "w_in": _permute_w_in(w_in[l]),
            "m_bias": m_bias,
            "dt_bias": dt_bias,
            "a_log": jnp.broadcast_to(ssd_a_log[l].reshape(N_DIR * HEADS, 1), (N_DIR * HEADS, LANES)),
            "d_skip": jnp.repeat(ssd_d[l], HD)[None, :],
            "ssd_conv_w": ssd_conv_w[l],
            "ssd_conv_b": ssd_conv_b[l][None, :],
            "w_out": w_out[l].astype(BF16),
            "w_pool": _block_diag4(w_pool[l]).astype(BF16),
            "pool_scale": pool_scale[l][None, :],
            "mnorm": mlstm_norm_g[l][None, :],
            "snorm": ssd_norm_g[l][None, :],
            "wsp": jnp.transpose(w_spatial[l], (1, 0, 2)).reshape(G_CHUNK, HEADS * G_CHUNK).astype(BF16),
            "bsp": jnp.repeat(b_spatial[l].T, HD, axis=-1),
            "ln1_g": ln1_g[l][None, :], "ln1_b": ln1_b[l][None, :],
            "ffn_w_up": ffn_w_up[l].astype(BF16),
            "ffn_conv_w": ffn_conv_w[l],
            "ffn_conv_b": ffn_conv_b[l][None, :],
            "ffn_w_down": ffn_w_down[l].astype(BF16),
            "ln2_g": ln2_g[l][None, :], "ln2_b": ln2_b[l][None, :],
        }
        y_p, (cout, mout), hs = _layer(y_p, None, mod[l], p, zero_m, zero_s, n_ctx, t_ctx, ctx_blk, ctx_cond, consts)
        cc, nn, mm = _unpack_mlstm_state(cout, mout)
        new_c.append(cc)
        new_n.append(nn)
        new_m.append(mm)
        new_s.append(_unpack_ssd_state(hs))
        lat_m = _pack_mlstm_state(state_mlstm_c[:, l], state_mlstm_n[:, l], state_mlstm_m[:, l])
        y_s, _, _ = _layer(y_s, pos if l == 0 else None, mod[l], p, lat_m, _pack_ssd_state(state_ssd[:, l]),
                           n_lat, t_lat, lat_blk, lat_cond, consts)
    return (y_p.reshape(n_ctx, t_ctx, D_MODEL), y_s.reshape(n_lat, t_lat, D_MODEL),
            jnp.stack(new_c, 1), jnp.stack(new_n, 1), jnp.stack(new_m, 1), jnp.stack(new_s, 1))
```

```python
import functools

import numpy as np
import jax
import jax.numpy as jnp
from jax import lax
from jax.experimental import pallas as pl
from jax.experimental.pallas import tpu as pltpu

F32 = jnp.float32
BF16 = jnp.bfloat16

D_MODEL = 1024
DEPTH = 2
GRID_W = 64
POS_BASE = 10000.0
W_GROUP = 256
N_DIR = 2
HEADS = 4
HD = 64
G_CHUNK = 128
S_STATE = 128
S_XBC = 768
D_FF = 2816
FF_CHUNK = 256
ALPHA = (2 * DEPTH) ** 0.25
LN_EPS = 1e-5
N_COND = 16
CTX_COND = 8
LANES = 128
SUBLANES = 8
CHUNK = LANES
HS = HEADS * CHUNK
SCAN_RADIX = 8
QKV_W = 768
MISC_W = 1280
IN_W = QKV_W + MISC_W + S_XBC + LANES
M_EXP_W = HS + 4 * W_GROUP
S_EXP_W = HS + 3 * W_GROUP
M_SEL_K = 3 * 48
S_SEL_K = 3 * 32
VMEM_LIMIT = 56 * 1024 * 1024


def _dot(a, b):
    return jnp.dot(a, b, preferred_element_type=F32)


def _dot_nt(a, b):
    return lax.dot_general(a, b, (((1,), (1,)), ((), ())), preferred_element_type=F32)


def _dot_tn(a, b):
    return lax.dot_general(a, b, (((0,), (0,)), ((), ())), preferred_element_type=F32)


def _split3(x):
    hi = x.astype(BF16)
    r1 = x - hi.astype(F32)
    mid = r1.astype(BF16)
    lo = (r1 - mid.astype(F32)).astype(BF16)
    return hi, mid, lo


def _dot_sel2(x, sel):
    hi = x.astype(BF16)
    lo = (x - hi.astype(F32)).astype(BF16)
    return _dot(hi, sel) + _dot(lo, sel)


def _expand(rows_f32, sel):
    hi, mid, lo = _split3(rows_f32)
    return _dot_tn(jnp.concatenate([hi, mid, lo], axis=0), sel)


def _ln(x):
    mu = jnp.mean(x, -1, keepdims=True)
    xc = x - mu
    var = jnp.mean(xc * xc, -1, keepdims=True)
    return xc * lax.rsqrt(var + LN_EPS)


def _sigmoid(x):
    return 1.0 / (1.0 + jnp.exp(-x))


def _silu(x):
    return x * _sigmoid(x)


def _softplus(x):
    return jnp.maximum(x, 0.0) + jnp.log1p(jnp.exp(-jnp.abs(x)))


def _log_sigmoid(x):
    return jnp.minimum(x, 0.0) - jnp.log1p(jnp.exp(-jnp.abs(x)))


def _iota(shape, dim):
    return lax.broadcasted_iota(jnp.int32, shape, dim)


def _lane_scan(x, op, ident, reverse):
    pos = _iota(x.shape, 1)
    k = 1
    while k < LANES:
        acc = x
        for mult in range(1, SCAN_RADIX):
            s = k * mult
            if s >= LANES:
                break
            if reverse:
                shifted = pltpu.roll(x, LANES - s, 1)
                valid = pos < LANES - s
            else:
                shifted = pltpu.roll(x, s, 1)
                valid = pos >= s
            acc = op(acc, jnp.where(valid, shifted, ident))
        x = acc
        k *= SCAN_RADIX
    return x


def _lane_bcast(x, lane):
    return jnp.broadcast_to(x[:, lane:lane + 1], x.shape)


def _stack_chunks(gt, r0, nb):
    return jnp.concatenate([gt[r0:r0 + SUBLANES, c * CHUNK:(c + 1) * CHUNK] for c in range(nb)], axis=0)


def _unstack_chunks(x, nb):
    return jnp.concatenate([x[c * SUBLANES:(c + 1) * SUBLANES] for c in range(nb)], axis=1)


def _head_rows(x, d):
    return jnp.concatenate([x[d * HEADS + h:d * HEADS + h + 1, :] for h in range(HEADS)], axis=1)


def _tile4(x):
    return jnp.concatenate([x, x, x, x], axis=0)


def _ada_kernel(cond_ref, w_ref, b_ref, o_ref):
    s = _silu(cond_ref[...]).astype(BF16)
    o_ref[0] = _dot(s, w_ref[0].astype(BF16)) + b_ref[0]


def _ada_call(cond, w_ada, b_ada):
    n_out = w_ada.shape[-1]
    tn = 1536
    return pl.pallas_call(
        _ada_kernel,
        grid=(DEPTH, n_out // tn),
        in_specs=[
            pl.BlockSpec((N_COND, D_MODEL), lambda l, n: (0, 0)),
            pl.BlockSpec((1, D_MODEL, tn), lambda l, n: (l, 0, n)),
            pl.BlockSpec((1, 1, tn), lambda l, n: (l, 0, n)),
        ],
        out_specs=pl.BlockSpec((1, N_COND, tn), lambda l, n: (l, 0, n)),
        out_shape=jax.ShapeDtypeStruct((DEPTH, N_COND, n_out), F32),
        compiler_params=pltpu.CompilerParams(
            dimension_semantics=("arbitrary", "arbitrary"), vmem_limit_bytes=VMEM_LIMIT),
        name="ada_mod",
    )(cond, w_ada, b_ada.reshape(DEPTH, 1, n_out))


def _inproj_kernel(*refs, has_pos):
    if has_pos:
        x_ref, pos_ref, mod_ref, w_ref, qkv_ref, misc_ref, xbc_ref, gates_ref, xres_ref = refs
        x = x_ref[...] + pos_ref[...]
        xres_ref[...] = x
    else:
        x_ref, mod_ref, w_ref, qkv_ref, misc_ref, xbc_ref, gates_ref = refs
        x = x_ref[...]
    shift = mod_ref[0, 0:1, :]
    scale = mod_ref[0, 1:2, :]
    h = (_ln(x) * (1.0 + scale) + shift).astype(BF16)
    qkv = _dot(h, w_ref[:, 0:QKV_W])
    qkv_ref[:, 0:256] = qkv[:, 0:256].astype(BF16)
    qkv_ref[:, 256:512] = (qkv[:, 256:512] * (HD ** -0.5)).astype(BF16)
    qkv_ref[:, 512:768] = qkv[:, 512:768].astype(BF16)
    misc_ref[...] = _dot(h, w_ref[:, QKV_W:QKV_W + MISC_W])
    xbc_ref[...] = _dot(h, w_ref[:, QKV_W + MISC_W:QKV_W + MISC_W + S_XBC])
    gates_ref[...] = _dot(h, w_ref[:, QKV_W + MISC_W + S_XBC:IN_W])


def _inproj_call(x, pos, mod_l, w_in_p, seq, rows_blk, cond_of_tile):
    rows = x.shape[0]
    nt = rows // rows_blk
    per_seq = seq // rows_blk
    has_pos = pos is not None
    row_spec = lambda w: pl.BlockSpec((rows_blk, w), lambda i: (i, 0))
    in_specs = [row_spec(D_MODEL)]
    args = [x]
    if has_pos:
        in_specs.append(pl.BlockSpec((rows_blk, D_MODEL), lambda i: (i % per_seq, 0)))
        args.append(pos)
    in_specs += [
        pl.BlockSpec((1, 6, D_MODEL), lambda i: (cond_of_tile(i), 0, 0)),
        pl.BlockSpec((D_MODEL, IN_W), lambda i: (0, 0)),
    ]
    args += [mod_l, w_in_p]
    out_specs = [row_spec(QKV_W), row_spec(MISC_W), row_spec(S_XBC), row_spec(LANES)]
    out_shape = [jax.ShapeDtypeStruct((rows, QKV_W), BF16), jax.ShapeDtypeStruct((rows, MISC_W), F32),
                 jax.ShapeDtypeStruct((rows, S_XBC), F32), jax.ShapeDtypeStruct((rows, LANES), F32)]
    if has_pos:
        out_specs.append(row_spec(D_MODEL))
        out_shape.append(jax.ShapeDtypeStruct((rows, D_MODEL), F32))
    return pl.pallas_call(
        functools.partial(_inproj_kernel, has_pos=has_pos),
        grid=(nt,), in_specs=in_specs, out_specs=out_specs, out_shape=out_shape,
        compiler_params=pltpu.CompilerParams(dimension_semantics=("arbitrary",), vmem_limit_bytes=VMEM_LIMIT),
        name="inproj",
    )(*args)


def _mlstm_prepare(gates, bias_row, sel, m_prev, d, nb, exp_ref, grow_ref):
    reverse = bool(d)
    last = 0 if reverse else LANES - 1
    gt = (gates + bias_row).T
    bcs = _lane_scan(_log_sigmoid(_stack_chunks(gt, 8, nb)), jnp.add, 0.0, reverse)
    g = _stack_chunks(gt, 0, nb) - bcs
    gmax = _lane_scan(g, jnp.maximum, -jnp.inf, reverse)
    g_last = _lane_bcast(gmax, last)
    b_last = _lane_bcast(bcs, last)
    m_prevs = [None] * nb
    m_lasts = [None] * nb
    for c in (range(nb - 1, -1, -1) if reverse else range(nb)):
        r = slice(c * SUBLANES, (c + 1) * SUBLANES)
        m_prevs[c] = m_prev
        m_lasts[c] = jnp.maximum(m_prev, g_last[r])
        m_prev = b_last[r] + m_lasts[c]
        grow_ref[d, c:c + 1, :] = _head_rows(g[r], d)
    mp = jnp.concatenate(m_prevs, axis=0)
    ml = jnp.concatenate(m_lasts, axis=0)
    m_q = jnp.maximum(mp, gmax)
    quantities = (-m_q, jnp.exp(mp - m_q), -(bcs + m_q), jnp.exp(g - ml), jnp.exp(mp - ml), jnp.zeros_like(mp))
    exp_ref[d] = _expand(jnp.concatenate([_unstack_chunks(q, nb) for q in quantities], axis=0), sel)
    return m_prev


def _mlstm_kernel(qkvf_ref, qkvb_ref, gf_ref, gb_ref, bias_ref, sel_ref, c0_ref, m0_ref,
                  hf_ref, hb_ref, cout_ref, mout_ref, caug_ref, m_ref, exp_ref, grow_ref, *, nb):
    j = pl.program_id(1)
    nblk = pl.num_programs(1)

    @pl.when(j == 0)
    def _():
        caug_ref[...] = c0_ref[0]
        m_ref[...] = m0_ref[0]

    lane = _iota((CHUNK, HS), 1)
    row = _iota((CHUNK, HS), 0)
    causal = ((lane % CHUNK) <= row, (lane % CHUNK) >= row)
    bd_hs = (_iota((HS, 256), 0) // CHUNK) == (_iota((HS, 256), 1) // HD)
    bd = (_iota((256, 256), 0) // HD) == (_iota((256, 256), 1) // HD)
    bd2 = jnp.concatenate([bd, bd], axis=1)
    ones_rows = jnp.ones((CHUNK, 256), BF16)
    zero = jnp.zeros((), BF16)

    qkv_refs = (qkvf_ref, qkvb_ref)
    g_refs = (gf_ref, gb_ref)
    h_refs = (hf_ref, hb_ref)
    for d in range(N_DIR):
        m_ref[d] = _mlstm_prepare(g_refs[d][...], bias_ref[...], sel_ref[d], m_ref[d], d, nb, exp_ref, grow_ref)

    for i in range(nb):
        for d in range(N_DIR):
            c = (nb - 1 - i) if d else i
            rows = slice(c * CHUNK, (c + 1) * CHUNK)
            q = qkv_refs[d][rows, 0:256]
            k = qkv_refs[d][rows, 256:512]
            v = qkv_refs[d][rows, 512:768]
            s = _dot_nt(q, jnp.where(bd_hs, _tile4(k), zero))
            expo = exp_ref[d, rows, 0:HS] + grow_ref[d, c:c + 1, :]
            p = s * jnp.exp(jnp.where(causal[d], expo, -jnp.inf))
            den_intra = jnp.concatenate(
                [jnp.broadcast_to(jnp.sum(p[:, h * CHUNK:(h + 1) * CHUNK], axis=-1, keepdims=True), (CHUNK, HD))
                 for h in range(HEADS)], axis=1)
            num_intra = _dot(p.astype(BF16), jnp.where(bd_hs, _tile4(v), zero))
            qc = _dot(q, caug_ref[d].astype(BF16))
            ew = exp_ref[d, rows, HS:HS + 256]
            num = num_intra + ew * qc[:, 0:256]
            den = den_intra + ew * qc[:, 256:512]
            h_refs[d][rows, :] = num / jnp.maximum(jnp.abs(den), jnp.exp(exp_ref[d, rows, HS + 256:HS + 512]))
            kw = (k.astype(F32) * exp_ref[d, rows, HS + 512:HS + 768]).astype(BF16)
            u = _dot_tn(kw, jnp.concatenate([v, ones_rows], axis=1))
            wc = exp_ref[d, c * CHUNK:c * CHUNK + 1, HS + 768:HS + 1024]
            caug_ref[d] = jnp.concatenate([wc, wc], axis=1) * caug_ref[d] + jnp.where(bd2, u, 0.0)

    @pl.when(j == nblk - 1)
    def _():
        mout_ref[0] = m_ref[...]
        for d in range(N_DIR):
            for h in range(HEADS):
                r = slice(h * HD, (h + 1) * HD)
                cout_ref[0, d, r, 0:HD] = caug_ref[d, r, h * HD:(h + 1) * HD]
                cout_ref[0, d, r, HD:2 * HD] = caug_ref[d, r, 256 + h * HD:256 + (h + 1) * HD]


def _mlstm_call(qkv, gates, bias_row, sel, c0, m0, batch, seq, rows_blk):
    rows = qkv.shape[0]
    nblk = seq // rows_blk
    nb = rows_blk // CHUNK
    fwd = lambda w: pl.BlockSpec((rows_blk, w), lambda b, j: (b * nblk + j, 0))
    bwd = lambda w: pl.BlockSpec((rows_blk, w), lambda b, j: (b * nblk + nblk - 1 - j, 0))
    per_b = lambda shape: pl.BlockSpec((1,) + shape, lambda b, j: (b,) + tuple(0 for _ in shape))
    return pl.pallas_call(
        functools.partial(_mlstm_kernel, nb=nb),
        grid=(batch, nblk),
        in_specs=[fwd(QKV_W), bwd(QKV_W), fwd(LANES), bwd(LANES),
                  pl.BlockSpec((1, LANES), lambda b, j: (0, 0)),
                  pl.BlockSpec((N_DIR, M_SEL_K, M_EXP_W), lambda b, j: (0, 0, 0)),
                  per_b((N_DIR, 256, 512)), per_b((N_DIR, SUBLANES, LANES))],
        out_specs=[fwd(256), bwd(256), per_b((N_DIR, 256, LANES)), per_b((N_DIR, SUBLANES, LANES))],
        out_shape=[jax.ShapeDtypeStruct((rows, 256), F32), jax.ShapeDtypeStruct((rows, 256), F32),
                   jax.ShapeDtypeStruct((batch, N_DIR, 256, LANES), F32),
                   jax.ShapeDtypeStruct((batch, N_DIR, SUBLANES, LANES), F32)],
        scratch_shapes=[pltpu.VMEM((N_DIR, 256, 512), F32), pltpu.VMEM((N_DIR, SUBLANES, LANES), F32),
                        pltpu.VMEM((N_DIR, rows_blk, M_EXP_W), F32), pltpu.VMEM((N_DIR, SUBLANES, HS), F32)],
        compiler_params=pltpu.CompilerParams(
            dimension_semantics=("arbitrary", "arbitrary"), vmem_limit_bytes=VMEM_LIMIT),
        name="mlstm",
    )(qkv, qkv, gates, gates, bias_row, sel, c0, m0)


def _ssd_conv_kernel(x_ref, prev_ref, next_ref, cw_ref, cb_ref, xs_ref, bc_ref, *, per_seq):
    x = x_ref[...]
    n_rows = x.shape[0]
    blk = pl.program_id(0) % per_seq
    rid = _iota(x.shape, 0)
    xp = jnp.where(blk == 0, 0.0, prev_ref[SUBLANES - 1:SUBLANES, :])
    xn = jnp.where(blk == per_seq - 1, 0.0, next_ref[0:1, :])
    x_prev = jnp.where(rid == 0, xp, pltpu.roll(x, 1, 0))
    x_next = jnp.where(rid == n_rows - 1, xn, pltpu.roll(x, n_rows - 1, 0))
    act = _silu(cw_ref[0:1, :] * x_prev + cw_ref[1:2, :] * x + cw_ref[2:3, :] * x_next + cb_ref[...])
    xs_ref[...] = act[:, 0:256]
    bc_ref[...] = act[:, 256:768].astype(BF16)


def _ssd_conv_call(xbc, conv_w, conv_b, seq, rows_blk):
    rows = xbc.shape[0]
    per_seq = seq // rows_blk
    sub_per_blk = rows_blk // SUBLANES
    n_sub = rows // SUBLANES
    row_spec = lambda w: pl.BlockSpec((rows_blk, w), lambda i: (i, 0))
    return pl.pallas_call(
        functools.partial(_ssd_conv_kernel, per_seq=per_seq),
        grid=(rows // rows_blk,),
        in_specs=[row_spec(S_XBC),
                  pl.BlockSpec((SUBLANES, S_XBC), lambda i: (jnp.maximum(i * sub_per_blk - 1, 0), 0)),
                  pl.BlockSpec((SUBLANES, S_XBC), lambda i: (jnp.minimum((i + 1) * sub_per_blk, n_sub - 1), 0)),
                  pl.BlockSpec((3, S_XBC), lambda i: (0, 0)), pl.BlockSpec((1, S_XBC), lambda i: (0, 0))],
        out_specs=[row_spec(256), row_spec(512)],
        out_shape=[jax.ShapeDtypeStruct((rows, 256), F32), jax.ShapeDtypeStruct((rows, 512), BF16)],
        compiler_params=pltpu.CompilerParams(dimension_semantics=("arbitrary",), vmem_limit_bytes=VMEM_LIMIT),
        name="ssd_conv",
    )(xbc, xbc, xbc, conv_w, conv_b)


def _ssd_prepare(gates, dtb_ref, alog_ref, sel, d, nb, exp_ref, grow_ref, dtrow_ref):
    reverse = bool(d)
    last = 0 if reverse else LANES - 1
    gt = (gates + dtb_ref[...]).T
    a = -jnp.exp(alog_ref[...])
    dt = _softplus(_stack_chunks(gt, 16, nb))
    lc = _lane_scan(dt * jnp.concatenate([a] * nb, axis=0), jnp.add, 0.0, reverse)
    lc_last = _lane_bcast(lc, last)
    for c in range(nb):
        r = slice(c * SUBLANES, (c + 1) * SUBLANES)
        grow_ref[d, c:c + 1, :] = _head_rows(-lc[r], d)
        dtrow_ref[d, c:c + 1, :] = _head_rows(dt[r], d)
    quantities = (lc, jnp.exp(lc), jnp.exp(lc_last - lc) * dt, jnp.exp(lc_last))
    exp_ref[d] = _expand(jnp.concatenate([_unstack_chunks(q, nb) for q in quantities], axis=0), sel)


def _ssd_kernel(xsf_ref, bcf_ref, xsb_ref, bcb_ref, gf_ref, gb_ref, dtb_ref, sel_ref, alog_ref, dskip_ref, h0_ref,
                yf_ref, yb_ref, hout_ref, h_ref, exp_ref, grow_ref, dtrow_ref, *, nb):
    j = pl.program_id(1)
    nblk = pl.num_programs(1)

    @pl.when(j == 0)
    def _():
        h_ref[...] = h0_ref[0]

    lane = _iota((CHUNK, HS), 1)
    row = _iota((CHUNK, HS), 0)
    causal = ((lane % CHUNK) <= row, (lane % CHUNK) >= row)
    r_hs = _iota((HS, 256), 0)
    c_hs = _iota((HS, 256), 1)
    bd_hs = (r_hs // CHUNK) == (c_hs // HD)
    gmask_b = (r_hs // CHUNK // 2) == (c_hs // S_STATE)
    gmask_h = (_iota((256, 256), 0) // S_STATE) == (_iota((256, 256), 1) // HD // 2)
    zero = jnp.zeros((), BF16)

    xs_refs = (xsf_ref, xsb_ref)
    bc_refs = (bcf_ref, bcb_ref)
    g_refs = (gf_ref, gb_ref)
    y_refs = (yf_ref, yb_ref)
    for d in range(N_DIR):
        _ssd_prepare(g_refs[d][...], dtb_ref, alog_ref, sel_ref[d], d, nb, exp_ref, grow_ref, dtrow_ref)

    for i in range(nb):
        for d in range(N_DIR):
            c = (nb - 1 - i) if d else i
            rows = slice(c * CHUNK, (c + 1) * CHUNK)
            xs = xs_refs[d][rows, :]
            bm = bc_refs[d][rows, 0:256]
            cm = bc_refs[d][rows, 256:512]
            cb = _dot_nt(cm, jnp.where(gmask_b, _tile4(bm), zero))
            expo = exp_ref[d, rows, 0:HS] + grow_ref[d, c:c + 1, :]
            dmat = jnp.exp(jnp.where(causal[d], expo, -jnp.inf)) * dtrow_ref[d, c:c + 1, :]
            p = (cb * dmat).astype(BF16)
            xbd = jnp.where(bd_hs, _tile4(xs.astype(BF16)), zero)
            y = _dot(p, xbd) + exp_ref[d, rows, HS:HS + 256] * _dot(cm, h_ref[d].astype(BF16))
            if d == 0:
                y = y + dskip_ref[...] * xs
            y_refs[d][rows, :] = y
            xw = (xs * exp_ref[d, rows, HS + 256:HS + 512]).astype(BF16)
            u = _dot_tn(bm, xw)
            da = exp_ref[d, c * CHUNK:c * CHUNK + 1, HS + 512:HS + 768]
            h_ref[d] = da * h_ref[d] + jnp.where(gmask_h, u, 0.0)

    @pl.when(j == nblk - 1)
    def _():
        hout_ref[0] = h_ref[...]


def _ssd_call(xs, bc, gates, dtb_row, sel, alog8, dskip_exp, h0, batch, seq, rows_blk):
    rows = xs.shape[0]
    nblk = seq // rows_blk
    nb = rows_blk // CHUNK
    fblk = lambda b, j: b * nblk + j
    bblk = lambda b, j: b * nblk + nblk - 1 - j
    main = lambda f, w: pl.BlockSpec((rows_blk, w), lambda b, j: (f(b, j), 0))
    const = lambda shape: pl.BlockSpec(shape, lambda b, j: tuple(0 for _ in shape))
    return pl.pallas_call(
        functools.partial(_ssd_kernel, nb=nb),
        grid=(batch, nblk),
        in_specs=[main(fblk, 256), main(fblk, 512), main(bblk, 256), main(bblk, 512),
                  main(fblk, LANES), main(bblk, LANES),
                  const((1, LANES)), const((N_DIR, S_SEL_K, S_EXP_W)), const((SUBLANES, LANES)), const((1, 256)),
                  pl.BlockSpec((1, N_DIR, 256, 256), lambda b, j: (b, 0, 0, 0))],
        out_specs=[main(fblk, 256), main(bblk, 256),
                   pl.BlockSpec((1, N_DIR, 256, 256), lambda b, j: (b, 0, 0, 0))],
        out_shape=[jax.ShapeDtypeStruct((rows, 256), F32), jax.ShapeDtypeStruct((rows, 256), F32),
                   jax.ShapeDtypeStruct((batch, N_DIR, 256, 256), F32)],
        scratch_shapes=[pltpu.VMEM((N_DIR, 256, 256), F32), pltpu.VMEM((N_DIR, rows_blk, S_EXP_W), F32),
                        pltpu.VMEM((N_DIR, SUBLANES, HS), F32), pltpu.VMEM((N_DIR, SUBLANES, HS), F32)],
        compiler_params=pltpu.CompilerParams(
            dimension_semantics=("arbitrary", "arbitrary"), vmem_limit_bytes=VMEM_LIMIT),
        name="ssd",
    )(xs, bc, xs, bc, gates, gates, dtb_row, sel, alog8, dskip_exp, h0)


def _mix_kernel(x_ref, hf_ref, hb_ref, misc_ref, pprev_ref, pnext_ref, yf_ref, yb_ref, mod_ref,
                wout_ref, wpool_ref, pscale_ref, mnorm_ref, snorm_ref, wsp_ref, bsp_ref, ln_g_ref, ln_b_ref,
                mean_sel_ref, o_ref, cat_ref, pad_ref, *, seq, per_seq):
    rows = x_ref.shape[0]
    i = pl.program_id(0)
    hs = hf_ref[...] + hb_ref[...]
    mu = _dot_sel2(hs, mean_sel_ref[...])
    hc = hs - mu
    var = _dot_sel2(hc * hc, mean_sel_ref[...])
    y_m = _sigmoid(misc_ref[:, 0:256]) * (hc * lax.rsqrt(var + LN_EPS) * mnorm_ref[...])
    cat_ref[:, 0:256] = y_m.astype(BF16)
    blk = i % per_seq
    xp = misc_ref[:, 256:512]
    pad_ref[0:SUBLANES, :] = jnp.where(blk == 0, 0.0, pprev_ref[...])
    pad_ref[SUBLANES:SUBLANES + rows, :] = xp
    pad_ref[SUBLANES + rows:2 * SUBLANES + rows, :] = jnp.where(blk == per_seq - 1, 0.0, pnext_ref[...])
    n_pad = rows + 2 * SUBLANES
    back = lambda a, k: pltpu.roll(a, k, 0)
    ahead = lambda a, k: pltpu.roll(a, n_pad - k, 0)
    s2 = pad_ref[...]
    s2 = s2 + back(s2, 1)
    s4 = s2 + back(s2, 2)
    lo_half = _iota((n_pad, LANES), 1) < HD
    s4_hi = s4[:, LANES:2 * LANES]
    s8 = s4_hi + back(s4_hi, 4)
    s16 = s8 + back(s8, 8)
    wsum = jnp.concatenate([jnp.where(lo_half, s2[:, 0:LANES], ahead(s4[:, 0:LANES], 1)),
                            jnp.where(lo_half, ahead(s8, 3), ahead(s16, 7))], axis=1)
    wsum = wsum[SUBLANES:SUBLANES + rows, :]
    win = jnp.left_shift(2, _iota((SUBLANES, 256), 1) // HD)
    half = win // 2
    pooled = wsum * (1.0 / win[0:1, :].astype(F32)) - xp

    def edge(r0, t0):
        t = t0 + _iota((SUBLANES, 256), 0)
        cnt = jnp.clip(t - half + win, 0, seq) - jnp.clip(t - half, 0, seq)
        return wsum[r0:r0 + SUBLANES, :] / cnt.astype(F32) - xp[r0:r0 + SUBLANES, :]

    top = jnp.where(blk == 0, edge(0, 0), pooled[0:SUBLANES, :])
    bot = jnp.where(blk == per_seq - 1, edge(rows - SUBLANES, seq - SUBLANES), pooled[rows - SUBLANES:rows, :])
    pooled = jnp.concatenate([top, pooled[SUBLANES:rows - SUBLANES, :], bot], axis=0)
    y_p = _dot(pooled.astype(BF16), wpool_ref[...]) * pscale_ref[...]
    cat_ref[:, 256:512] = y_p.astype(BF16)
    vn = _ln(misc_ref[:, 768:1024]).astype(BF16)
    bd = (_iota((512, 256), 0) // G_CHUNK) == (_iota((512, 256), 1) // HD)
    for c in range(rows // G_CHUNK):
        r = slice(c * G_CHUNK, (c + 1) * G_CHUNK)
        vbd = jnp.where(bd, _tile4(vn[r, :]), jnp.zeros((), BF16))
        mixed = _dot(wsp_ref[...], vbd) + bsp_ref[...]
        cat_ref[r, 512:768] = (misc_ref[r, 512:768] * mixed).astype(BF16)
    y = (yf_ref[...] + yb_ref[...]) * _silu(misc_ref[:, 1024:1280])
    for g in range(2):
        yg = y[:, g * LANES:(g + 1) * LANES]
        yg = yg * lax.rsqrt(jnp.mean(yg * yg, -1, keepdims=True) + LN_EPS)
        cat_ref[:, 768 + g * LANES:768 + (g + 1) * LANES] = (yg * snorm_ref[:, g * LANES:(g + 1) * LANES]).astype(BF16)
    mix = _dot(cat_ref[...], wout_ref[...])
    gate = mod_ref[0, 2:3, :]
    o_ref[...] = _ln(ALPHA * x_ref[...] + gate * mix) * ln_g_ref[...] + ln_b_ref[...]


def _mix_call(x, hf, hb, misc, yf, yb, mod_l, w_out, w_pool_bd, pool_scale, mnorm, snorm, wsp_cat, bsp_exp,
              ln_g, ln_b, mean_sel, seq, rows_blk, cond_of_tile):
    rows = x.shape[0]
    nt = rows // rows_blk
    per_seq = seq // rows_blk
    sub_per_blk = rows_blk // SUBLANES
    n_sub = rows // SUBLANES
    row_spec = lambda w: pl.BlockSpec((rows_blk, w), lambda i: (i, 0))
    const = lambda shape: pl.BlockSpec(shape, lambda i: tuple(0 for _ in shape))
    return pl.pallas_call(
        functools.partial(_mix_kernel, seq=seq, per_seq=per_seq),
        grid=(nt,),
        in_specs=[row_spec(D_MODEL), row_spec(256), row_spec(256), row_spec(MISC_W),
                  pl.BlockSpec((SUBLANES, 256), lambda i: (jnp.maximum(i * sub_per_blk - 1, 0), 1)),
                  pl.BlockSpec((SUBLANES, 256), lambda i: (jnp.minimum((i + 1) * sub_per_blk, n_sub - 1), 1)),
                  row_spec(256), row_spec(256),
                  pl.BlockSpec((1, 6, D_MODEL), lambda i: (cond_of_tile(i), 0, 0)),
                  const((D_MODEL, D_MODEL)), const((256, 256)), const((1, 256)), const((1, 256)), const((1, 256)),
                  const((G_CHUNK, 512)), const((G_CHUNK, 256)), const((1, D_MODEL)), const((1, D_MODEL)),
                  const((256, 256))],
        out_specs=row_spec(D_MODEL),
        out_shape=jax.ShapeDtypeStruct((rows, D_MODEL), F32),
        scratch_shapes=[pltpu.VMEM((rows_blk, D_MODEL), BF16), pltpu.VMEM((rows_blk + 2 * SUBLANES, 256), F32)],
        compiler_params=pltpu.CompilerParams(dimension_semantics=("arbitrary",), vmem_limit_bytes=VMEM_LIMIT),
        name="mix_outproj",
    )(x, hf, hb, misc, misc, misc, yf, yb, mod_l, w_out, w_pool_bd, pool_scale, mnorm, snorm, wsp_cat, bsp_exp,
      ln_g, ln_b, mean_sel)


def _ffn_kernel(x_ref, xprev_ref, xnext_ref, mod_ref, wup_ref, cw_ref, cb_ref, wdown_ref, ln_g_ref, ln_b_ref,
                o_ref, h_ref, act_ref, *, per_seq):
    rows = x_ref.shape[0]
    blk = pl.program_id(0) % per_seq
    shift = mod_ref[0, 3:4, :]
    scale = mod_ref[0, 4:5, :]
    gate = mod_ref[0, 5:6, :]
    modulate = lambda v: (_ln(v) * (1.0 + scale) + shift).astype(BF16)
    x = x_ref[...]
    base = 2 * SUBLANES
    n_pad = rows + 2 * base
    hp = modulate(jnp.concatenate([xprev_ref[...], xprev_ref[...]], axis=0))
    hn = modulate(jnp.concatenate([xnext_ref[...], xnext_ref[...]], axis=0))
    h_ref[0:base, :] = jnp.where(blk > 0, hp, jnp.zeros((), BF16))
    h_ref[base:base + rows, :] = modulate(x)
    h_ref[base + rows:n_pad, :] = jnp.where(blk < per_seq - 1, hn, jnp.zeros((), BF16))
    h = h_ref[...]

    def conv(u, col):
        up = pltpu.roll(u, 1, 0)[base:base + rows, :]
        un = pltpu.roll(u, n_pad - 1, 0)[base:base + rows, :]
        return (cw_ref[0:1, col] * up + cw_ref[1:2, col] * u[base:base + rows, :] + cw_ref[2:3, col] * un
                + cb_ref[:, col])

    for c in range(D_FF // FF_CHUNK):
        cg = slice(c * FF_CHUNK, (c + 1) * FF_CHUNK)
        cv = slice(D_FF + c * FF_CHUNK, D_FF + (c + 1) * FF_CHUNK)
        g = conv(_dot(h, wup_ref[:, cg]), cg)
        val = conv(_dot(h, wup_ref[:, cv]), cv)
        act_ref[:, cg] = (_silu(g) * val).astype(BF16)
    ffn = _dot(act_ref[...], wdown_ref[...])
    o_ref[...] = _ln(ALPHA * x + gate * ffn) * ln_g_ref[...] + ln_b_ref[...]


def _ffn_call(x, mod_l, w_up, conv_w, conv_b, w_down, ln_g, ln_b, seq, rows_blk, cond_of_tile):
    rows = x.shape[0]
    nt = rows // rows_blk
    per_seq = seq // rows_blk
    sub_per_blk = rows_blk // SUBLANES
    n_sub = rows // SUBLANES
    row_spec = lambda w: pl.BlockSpec((rows_blk, w), lambda i: (i, 0))
    const = lambda shape: pl.BlockSpec(shape, lambda i: tuple(0 for _ in shape))
    return pl.pallas_call(
        functools.partial(_ffn_kernel, per_seq=per_seq),
        grid=(nt,),
        in_specs=[row_spec(D_MODEL),
                  pl.BlockSpec((SUBLANES, D_MODEL), lambda i: (jnp.maximum(i * sub_per_blk - 1, 0), 0)),
                  pl.BlockSpec((SUBLANES, D_MODEL), lambda i: (jnp.minimum((i + 1) * sub_per_blk, n_sub - 1), 0)),
                  pl.BlockSpec((1, 6, D_MODEL), lambda i: (cond_of_tile(i), 0, 0)),
                  const((D_MODEL, 2 * D_FF)), const((3, 2 * D_FF)), const((1, 2 * D_FF)),
                  const((D_FF, D_MODEL)), const((1, D_MODEL)), const((1, D_MODEL))],
        out_specs=row_spec(D_MODEL),
        out_shape=jax.ShapeDtypeStruct((rows, D_MODEL), F32),
        scratch_shapes=[pltpu.VMEM((rows_blk + 4 * SUBLANES, D_MODEL), BF16), pltpu.VMEM((rows_blk, D_FF), BF16)],
        compiler_params=pltpu.CompilerParams(dimension_semantics=("arbitrary",), vmem_limit_bytes=VMEM_LIMIT),
        name="conv_ffn",
    )(x, x, x, mod_l, w_up, conv_w, conv_b, w_down, ln_g, ln_b)


def _grid_pos_embed(rows, dim):
    quarter = dim // 4
    freq = 1.0 / (POS_BASE ** (jnp.arange(quarter, dtype=F32) / quarter))
    r = jnp.repeat(jnp.arange(rows, dtype=F32), GRID_W)
    col = jnp.tile(jnp.arange(GRID_W, dtype=F32), rows)

    def enc(pos):
        ang = pos[:, None] * freq[None, :]
        return jnp.concatenate([jnp.sin(ang), jnp.cos(ang)], -1)

    return jnp.concatenate([enc(r), enc(col)], -1)


def _selector(n_quant, pad_rows, widths):
    sel = np.zeros((N_DIR, 3 * pad_rows, sum(widths)), np.float32)
    offs = np.concatenate([[0], np.cumsum(widths)])
    for d in range(N_DIR):
        for p in range(3):
            for q in range(n_quant):
                per_head = widths[q] // HEADS
                for h in range(HEADS):
                    lo = offs[q] + h * per_head
                    sel[d, p * pad_rows + q * SUBLANES + d * HEADS + h, lo:lo + per_head] = 1.0
    return jnp.asarray(sel, BF16)


def _constants():
    m_sel = _selector(5, 48, [HS, 256, 256, 256, 256])
    s_sel = _selector(4, 32, [HS, 256, 256, 256])
    mean_sel = np.kron(np.eye(HEADS, dtype=np.float32), np.full((HD, HD), 1.0 / HD, np.float32))
    return m_sel, s_sel, jnp.asarray(mean_sel, BF16)


def _permute_w_in(w):
    pad = jnp.zeros((D_MODEL, LANES - 24), w.dtype)
    return jnp.concatenate([w[:, 0:768], w[:, 768:1024], w[:, 1040:1296], w[:, 1296:1808], w[:, 1808:2064],
                            w[:, 2064:2832], w[:, 1024:1040], w[:, 2832:2840], pad], axis=1).astype(BF16)


def _block_diag4(blocks):
    a, b = blocks.shape[1:]
    eye = jnp.eye(HEADS, dtype=blocks.dtype)
    return (eye[:, None, :, None] * blocks[:, :, None, :]).reshape(HEADS * a, HEADS * b)


def _dir_head_rows(v):
    rep = jnp.broadcast_to(v[..., None], v.shape + (LANES,))
    zeros = jnp.zeros_like(rep[..., 0, :, :])
    return jnp.stack([jnp.concatenate([rep[..., 0, :, :], zeros], axis=-2),
                      jnp.concatenate([zeros, rep[..., 1, :, :]], axis=-2)], axis=-3)


def _pack_mlstm_state(c, n, m):
    eye = jnp.eye(HEADS, dtype=F32)
    cbd = (eye[None, None, :, None, :, None] * c[:, :, :, :, None, :]).reshape(c.shape[0], N_DIR, 256, 256)
    nbd = jnp.broadcast_to((eye[None, None, :, None, :, None] * n[:, :, :, :, None, None]),
                           (c.shape[0], N_DIR, HEADS, HD, HEADS, HD)).reshape(c.shape[0], N_DIR, 256, 256)
    return jnp.concatenate([cbd, nbd], axis=-1), _dir_head_rows(m)


def _unpack_mlstm_state(cout, mout):
    b = cout.shape[0]
    c = cout[..., 0:HD].reshape(b, N_DIR, HEADS, HD, HD)
    n = cout[..., HD].reshape(b, N_DIR, HEADS, HD)
    m = jnp.stack([mout[:, d, d * HEADS:(d + 1) * HEADS, 0] for d in range(N_DIR)], axis=1)
    return c, n, m


def _pack_ssd_state(s):
    b = s.shape[0]
    st = jnp.transpose(s, (0, 1, 4, 2, 3))
    grp = (jnp.arange(HEADS) // 2)[None, :] == jnp.arange(2)[:, None]
    full = st[:, :, None, :, :, :] * grp[None, None, :, None, :, None].astype(F32)
    return full.reshape(b, N_DIR, 256, 256)


def _unpack_ssd_state(hp):
    heads = [jnp.swapaxes(hp[:, :, (h // 2) * S_STATE:(h // 2 + 1) * S_STATE, h * HD:(h + 1) * HD], -1, -2)
             for h in range(HEADS)]
    return jnp.stack(heads, axis=2)


def _layer(x, pos, mod_l, p, m_state, s_state, batch, seq, rows_blk, cond_of_tile, consts):
    m_sel, s_sel, mean_sel = consts
    outs = _inproj_call(x, pos, mod_l, p["w_in"], seq, rows_blk, cond_of_tile)
    if pos is not None:
        qkv, misc, xbc, gates, x = outs
    else:
        qkv, misc, xbc, gates = outs
    hf, hb, cout, mout = _mlstm_call(qkv, gates, p["m_bias"], m_sel, m_state[0], m_state[1], batch, seq, rows_blk)
    xs, bc = _ssd_conv_call(xbc, p["ssd_conv_w"], p["ssd_conv_b"], seq, rows_blk)
    yf, yb, hs = _ssd_call(xs, bc, gates, p["dt_bias"], s_sel, p["a_log"], p["d_skip"], s_state, batch, seq,
                           rows_blk)
    x1 = _mix_call(x, hf, hb, misc, yf, yb, mod_l, p["w_out"], p["w_pool"], p["pool_scale"], p["mnorm"],
                   p["snorm"], p["wsp"], p["bsp"], p["ln1_g"], p["ln1_b"], mean_sel, seq, rows_blk, cond_of_tile)
    x2 = _ffn_call(x1, mod_l, p["ffn_w_up"], p["ffn_conv_w"], p["ffn_conv_b"], p["ffn_w_down"], p["ln2_g"],
                   p["ln2_b"], seq, rows_blk, cond_of_tile)
    return x2, (cout, mout), hs


def kernel(x_prompt, x_sample, state_mlstm_c, state_mlstm_n, state_mlstm_m, state_ssd, c, c_ctx, w_ada, b_ada, w_in, b_igate, b_fgate, mlstm_norm_g, w_pool, pool_scale, w_spatial, b_spatial, ssd_conv_w, ssd_conv_b, ssd_dt_bias, ssd_a_log, ssd_d, ssd_norm_g, w_out, ln1_g, ln1_b, ffn_w_up, ffn_conv_w, ffn_conv_b, ffn_w_down, ln2_g, ln2_b):
    n_ctx, t_ctx, _ = x_prompt.shape
    n_lat, t_lat, _ = x_sample.shape
    consts = _constants()
    pos = _grid_pos_embed(t_lat // GRID_W, D_MODEL)

    cond = jnp.zeros((N_COND, D_MODEL), F32).at[0:n_lat].set(c).at[CTX_COND].set(c_ctx)
    mod = _ada_call(cond, w_ada, b_ada).reshape(DEPTH, N_COND, 6, D_MODEL)

    ctx_blk = min(t_ctx, 256)
    lat_blk = min(t_lat, 512)
    lat_per_seq = t_lat // lat_blk
    ctx_cond = lambda i: CTX_COND
    lat_cond = lambda i: i // lat_per_seq

    y_p = x_prompt.reshape(n_ctx * t_ctx, D_MODEL)
    y_s = x_sample.reshape(n_lat * t_lat, D_MODEL)
    zero_m = (jnp.zeros((n_ctx, N_DIR, 256, 512), F32), jnp.zeros((n_ctx, N_DIR, SUBLANES, LANES), F32))
    zero_s = jnp.zeros((n_ctx, N_DIR, 256, 256), F32)
    new_c, new_n, new_m, new_s = [], [], [], []
    for l in range(DEPTH):
        m_bias = jnp.zeros((1, LANES), F32).at[0, 0:8].set(b_igate[l].reshape(-1)).at[0, 8:16].set(b_fgate[l].reshape(-1))
        dt_bias = jnp.zeros((1, LANES), F32).at[0, 16:24].set(ssd_dt_bias[l].reshape(-1))
        p = {
            "w_in": _permute_w_in(w_in[l]),
            "m_bias": m_bias,
            "dt_bias": dt_bias,
            "a_log": jnp.broadcast_to(ssd_a_log[l].reshape(N_DIR * HEADS, 1), (N_DIR * HEADS, LANES)),
            "d_skip": jnp.repeat(ssd_d[l], HD)[None, :],
            "ssd_conv_w": ssd_conv_w[l],
            "ssd_conv_b": ssd_conv_b[l][None, :],
            "w_out": w_out[l].astype(BF16),
            "w_pool": _block_diag4(w_pool[l]).astype(BF16),
            "pool_scale": pool_scale[l][None, :],
            "mnorm": mlstm_norm_g[l][None, :],
            "snorm": ssd_norm_g[l][None, :],
            "wsp": jnp.transpose(w_spatial[l], (1, 0, 2)).reshape(G_CHUNK, HEADS * G_CHUNK).astype(BF16),
            "bsp": jnp.repeat(b_spatial[l].T, HD, axis=-1),
            "ln1_g": ln1_g[l][None, :], "ln1_b": ln1_b[l][None, :],
            "ffn_w_up": ffn_w_up[l].astype(BF16),
            "ffn_conv_w": ffn_conv_w[l],
            "ffn_conv_b": ffn_conv_b[l][None, :],
            "ffn_w_down": ffn_w_down[l].astype(BF16),
            "ln2_g": ln2_g[l][None, :], "ln2_b": ln2_b[l][None, :],
        }
        y_p, (cout, mout), hs = _layer(y_p, None, mod[l], p, zero_m, zero_s, n_ctx, t_ctx, ctx_blk, ctx_cond, consts)
        cc, nn, mm = _unpack_mlstm_state(cout, mout)
        new_c.append(cc)
        new_n.append(nn)
        new_m.append(mm)
        new_s.append(_unpack_ssd_state(hs))
        lat_m = _pack_mlstm_state(state_mlstm_c[:, l], state_mlstm_n[:, l], state_mlstm_m[:, l])
        y_s, _, _ = _layer(y_s, pos if l == 0 else None, mod[l], p, lat_m, _pack_ssd_state(state_ssd[:, l]),
                           n_lat, t_lat, lat_blk, lat_cond, consts)
    return (y_p.reshape(n_ctx, t_ctx, D_MODEL), y_s.reshape(n_lat, t_lat, D_MODEL),
            jnp.stack(new_c, 1), jnp.stack(new_n, 1), jnp.stack(new_m, 1), jnp.stack(new_s, 1))
```

```python
import functools

import numpy as np
import jax
import jax.numpy as jnp
from jax import lax
from jax.experimental import pallas as pl
from jax.experimental.pallas import tpu as pltpu

F32 = jnp.float32
BF16 = jnp.bfloat16

D_MODEL = 1024
DEPTH = 2
GRID_W = 64
POS_BASE = 10000.0
W_GROUP = 256
N_DIR = 2
HEADS = 4
HD = 64
G_CHUNK = 128
S_STATE = 128
S_XBC = 768
D_FF = 2816
FF_CHUNK = 256
ALPHA = (2 * DEPTH) ** 0.25
LN_EPS = 1e-5
N_COND = 16
CTX_COND = 8
LANES = 128
SUBLANES = 8
CHUNK = LANES
HS = HEADS * CHUNK
SCAN_RADIX = 8
QKV_W = 768
MISC_W = 1280
IN_W = QKV_W + MISC_W + S_XBC + LANES
M_EXP_W = HS + 3 * W_GROUP
S_EXP_W = HS + 2 * W_GROUP
M_SEL_K = 3 * 32
S_SEL_K = 3 * 32
NB_MAX = 4
VMEM_LIMIT = 56 * 1024 * 1024


def _dot(a, b):
    return jnp.dot(a, b, preferred_element_type=F32)


def _dot_nt(a, b):
    return lax.dot_general(a, b, (((1,), (1,)), ((), ())), preferred_element_type=F32)


def _dot_tn(a, b):
    return lax.dot_general(a, b, (((0,), (0,)), ((), ())), preferred_element_type=F32)


def _split3(x):
    hi = x.astype(BF16)
    r1 = x - hi.astype(F32)
    mid = r1.astype(BF16)
    lo = (r1 - mid.astype(F32)).astype(BF16)
    return hi, mid, lo


def _dot_sel2(x, sel):
    hi = x.astype(BF16)
    lo = (x - hi.astype(F32)).astype(BF16)
    return _dot(hi, sel) + _dot(lo, sel)


def _expand(rows_f32, sel):
    hi, mid, lo = _split3(rows_f32)
    return _dot_tn(jnp.concatenate([hi, mid, lo], axis=0), sel)


def _ln(x):
    mu = jnp.mean(x, -1, keepdims=True)
    xc = x - mu
    var = jnp.mean(xc * xc, -1, keepdims=True)
    return xc * lax.rsqrt(var + LN_EPS)


def _sigmoid(x):
    return 1.0 / (1.0 + jnp.exp(-x))


def _silu(x):
    return x * _sigmoid(x)


def _softplus(x):
    return jnp.maximum(x, 0.0) + jnp.log1p(jnp.exp(-jnp.abs(x)))


def _log_sigmoid(x):
    return jnp.minimum(x, 0.0) - jnp.log1p(jnp.exp(-jnp.abs(x)))


def _iota(shape, dim):
    return lax.broadcasted_iota(jnp.int32, shape, dim)


def _lane_scan_pair(xf, xr, op, ident):
    pos_f = _iota(xf.shape, 1)
    pos_r = _iota(xr.shape, 1)
    k = 1
    while k < LANES:
        acc_f, acc_r = xf, xr
        for mult in range(1, SCAN_RADIX):
            s = k * mult
            if s >= LANES:
                break
            acc_f = op(acc_f, jnp.where(pos_f >= s, pltpu.roll(xf, s, 1), ident))
            acc_r = op(acc_r, jnp.where(pos_r < LANES - s, pltpu.roll(xr, LANES - s, 1), ident))
        xf, xr = acc_f, acc_r
        k *= SCAN_RADIX
    return xf, xr


def _lane_bcast(x, lane):
    return jnp.broadcast_to(x[:, lane:lane + 1], x.shape)


def _stack_chunks(gt, r0, nb):
    return jnp.concatenate([gt[r0:r0 + SUBLANES, c * CHUNK:(c + 1) * CHUNK] for c in range(nb)], axis=0)


def _unstack_chunks(x, nb):
    return jnp.concatenate([x[c * SUBLANES:(c + 1) * SUBLANES] for c in range(nb)], axis=1)


def _head_rows(x, d):
    return jnp.concatenate([x[d * HEADS + h:d * HEADS + h + 1, :] for h in range(HEADS)], axis=1)


def _head_scalar_row(x, d):
    lo = _iota((1, LANES), 1) < HD
    r = d * HEADS
    return jnp.concatenate([jnp.where(lo, x[r:r + 1, :], x[r + 1:r + 2, :]),
                            jnp.where(lo, x[r + 2:r + 3, :], x[r + 3:r + 4, :])], axis=1)


def _tile4(x):
    return jnp.concatenate([x, x, x, x], axis=0)


def _ada_kernel(cond_ref, w_ref, b_ref, o_ref):
    s = _silu(cond_ref[...]).astype(BF16)
    o_ref[0] = _dot(s, w_ref[0].astype(BF16)) + b_ref[0]


def _ada_call(cond, w_ada, b_ada):
    n_out = w_ada.shape[-1]
    tn = 1536
    return pl.pallas_call(
        _ada_kernel,
        grid=(DEPTH, n_out // tn),
        in_specs=[
            pl.BlockSpec((N_COND, D_MODEL), lambda l, n: (0, 0)),
            pl.BlockSpec((1, D_MODEL, tn), lambda l, n: (l, 0, n)),
            pl.BlockSpec((1, 1, tn), lambda l, n: (l, 0, n)),
        ],
        out_specs=pl.BlockSpec((1, N_COND, tn), lambda l, n: (l, 0, n)),
        out_shape=jax.ShapeDtypeStruct((DEPTH, N_COND, n_out), F32),
        compiler_params=pltpu.CompilerParams(
            dimension_semantics=("arbitrary", "arbitrary"), vmem_limit_bytes=VMEM_LIMIT),
        name="ada_mod",
    )(cond, w_ada, b_ada.reshape(DEPTH, 1, n_out))


def _inproj_kernel(*refs, has_pos):
    if has_pos:
        x_ref, pos_ref, mod_ref, w_ref, qkv_ref, misc_ref, xbc_ref, gates_ref, xres_ref = refs
        x = x_ref[...] + pos_ref[...]
        xres_ref[...] = x
    else:
        x_ref, mod_ref, w_ref, qkv_ref, misc_ref, xbc_ref, gates_ref = refs
        x = x_ref[...]
    shift = mod_ref[0, 0:1, :]
    scale = mod_ref[0, 1:2, :]
    h = (_ln(x) * (1.0 + scale) + shift).astype(BF16)
    qkv = _dot(h, w_ref[:, 0:QKV_W])
    qkv_ref[:, 0:256] = qkv[:, 0:256].astype(BF16)
    qkv_ref[:, 256:512] = (qkv[:, 256:512] * (HD ** -0.5)).astype(BF16)
    qkv_ref[:, 512:768] = qkv[:, 512:768].astype(BF16)
    misc_ref[...] = _dot(h, w_ref[:, QKV_W:QKV_W + MISC_W])
    xbc_ref[...] = _dot(h, w_ref[:, QKV_W + MISC_W:QKV_W + MISC_W + S_XBC])
    gates_ref[...] = _dot(h, w_ref[:, QKV_W + MISC_W + S_XBC:IN_W])


def _inproj_call(x, pos, mod_l, w_in_p, seq, rows_blk, cond_of_tile):
    rows = x.shape[0]
    nt = rows // rows_blk
    per_seq = seq // rows_blk
    has_pos = pos is not None
    row_spec = lambda w: pl.BlockSpec((rows_blk, w), lambda i: (i, 0))
    in_specs = [row_spec(D_MODEL)]
    args = [x]
    if has_pos:
        in_specs.append(pl.BlockSpec((rows_blk, D_MODEL), lambda i: (i % per_seq, 0)))
        args.append(pos)
    in_specs += [
        pl.BlockSpec((1, 6, D_MODEL), lambda i: (cond_of_tile(i), 0, 0)),
        pl.BlockSpec((D_MODEL, IN_W), lambda i: (0, 0)),
    ]
    args += [mod_l, w_in_p]
    out_specs = [row_spec(QKV_W), row_spec(MISC_W), row_spec(S_XBC), row_spec(LANES)]
    out_shape = [jax.ShapeDtypeStruct((rows, QKV_W), BF16), jax.ShapeDtypeStruct((rows, MISC_W), F32),
                 jax.ShapeDtypeStruct((rows, S_XBC), F32), jax.ShapeDtypeStruct((rows, LANES), F32)]
    if has_pos:
        out_specs.append(row_spec(D_MODEL))
        out_shape.append(jax.ShapeDtypeStruct((rows, D_MODEL), F32))
    return pl.pallas_call(
        functools.partial(_inproj_kernel, has_pos=has_pos),
        grid=(nt,), in_specs=in_specs, out_specs=out_specs, out_shape=out_shape,
        compiler_params=pltpu.CompilerParams(dimension_semantics=("arbitrary",), vmem_limit_bytes=VMEM_LIMIT),
        name="inproj",
    )(*args)


def _mlstm_stages(qkvf_ref, qkvb_ref, gf_ref, gb_ref, bias_ref, sel_ref, c0_ref, m0_ref,
                  hf_ref, hb_ref, cout_ref, mout_ref, caug_ref, m_ref, exp_ref, grow_ref, nb):
    lane = _iota((CHUNK, HS), 1)
    row = _iota((CHUNK, HS), 0)
    causal = ((lane % CHUNK) <= row, (lane % CHUNK) >= row)
    bd_hs = (_iota((HS, 256), 0) // CHUNK) == (_iota((HS, 256), 1) // HD)
    bd = (_iota((256, 256), 0) // HD) == (_iota((256, 256), 1) // HD)
    bd2 = jnp.concatenate([bd, bd], axis=1)
    ones_rows = jnp.ones((CHUNK, 256), BF16)
    zero = jnp.zeros((), BF16)

    qkv_refs = (qkvf_ref, qkvb_ref)
    h_refs = (hf_ref, hb_ref)

    def init():
        caug_ref[...] = c0_ref[0]
        m_ref[...] = m0_ref[0]

    def chunk(i, d):
        c = (nb - 1 - i) if d else i
        rows = slice(c * CHUNK, (c + 1) * CHUNK)
        q = qkv_refs[d][rows, 0:256]
        k = qkv_refs[d][rows, 256:512]
        v = qkv_refs[d][rows, 512:768]
        s = _dot_nt(q, jnp.where(bd_hs, _tile4(k), zero))
        expo = exp_ref[d, rows, 0:HS] + grow_ref[d, c:c + 1, :]
        p = s * jnp.exp(jnp.where(causal[d], expo, -jnp.inf))
        den_intra = jnp.concatenate(
            [jnp.broadcast_to(jnp.sum(p[:, h * CHUNK:(h + 1) * CHUNK], axis=-1, keepdims=True), (CHUNK, HD))
             for h in range(HEADS)], axis=1)
        num_intra = _dot(p.astype(BF16), jnp.where(bd_hs, _tile4(v), zero))
        qc = _dot(q, caug_ref[d].astype(BF16))
        ew = exp_ref[d, rows, HS:HS + 256]
        num = num_intra + ew * qc[:, 0:256]
        den = den_intra + ew * qc[:, 256:512]
        h_refs[d][rows, :] = num / jnp.maximum(jnp.abs(den), jnp.exp(exp_ref[d, rows, HS + 256:HS + 512]))
        kw = (k.astype(F32) * exp_ref[d, rows, HS + 512:HS + 768]).astype(BF16)
        u = _dot_tn(kw, jnp.concatenate([v, ones_rows], axis=1))
        wc = grow_ref[d, NB_MAX + c:NB_MAX + c + 1, 0:256]
        caug_ref[d] = jnp.concatenate([wc, wc], axis=1) * caug_ref[d] + jnp.where(bd2, u, 0.0)

    def final():
        mout_ref[0] = m_ref[...]
        for d in range(N_DIR):
            for h in range(HEADS):
                r = slice(h * HD, (h + 1) * HD)
                cout_ref[0, d, r, 0:HD] = caug_ref[d, r, h * HD:(h + 1) * HD]
                cout_ref[0, d, r, HD:2 * HD] = caug_ref[d, r, 256 + h * HD:256 + (h + 1) * HD]

    return init, chunk, final


def _ssd_conv_kernel(x_ref, prev_ref, next_ref, cw_ref, cb_ref, xs_ref, bc_ref, *, per_seq):
    x = x_ref[...]
    n_rows = x.shape[0]
    blk = pl.program_id(0) % per_seq
    rid = _iota(x.shape, 0)
    xp = jnp.where(blk == 0, 0.0, prev_ref[SUBLANES - 1:SUBLANES, :])
    xn = jnp.where(blk == per_seq - 1, 0.0, next_ref[0:1, :])
    x_prev = jnp.where(rid == 0, xp, pltpu.roll(x, 1, 0))
    x_next = jnp.where(rid == n_rows - 1, xn, pltpu.roll(x, n_rows - 1, 0))
    act = _silu(cw_ref[0:1, :] * x_prev + cw_ref[1:2, :] * x + cw_ref[2:3, :] * x_next + cb_ref[...])
    xs_ref[...] = act[:, 0:256]
    bc_ref[...] = act[:, 256:768].astype(BF16)


def _ssd_conv_call(xbc, conv_w, conv_b, seq, rows_blk):
    rows = xbc.shape[0]
    per_seq = seq // rows_blk
    sub_per_blk = rows_blk // SUBLANES
    n_sub = rows // SUBLANES
    row_spec = lambda w: pl.BlockSpec((rows_blk, w), lambda i: (i, 0))
    return pl.pallas_call(
        functools.partial(_ssd_conv_kernel, per_seq=per_seq),
        grid=(rows // rows_blk,),
        in_specs=[row_spec(S_XBC),
                  pl.BlockSpec((SUBLANES, S_XBC), lambda i: (jnp.maximum(i * sub_per_blk - 1, 0), 0)),
                  pl.BlockSpec((SUBLANES, S_XBC), lambda i: (jnp.minimum((i + 1) * sub_per_blk, n_sub - 1), 0)),
                  pl.BlockSpec((3, S_XBC), lambda i: (0, 0)), pl.BlockSpec((1, S_XBC), lambda i: (0, 0))],
        out_specs=[row_spec(256), row_spec(512)],
        out_shape=[jax.ShapeDtypeStruct((rows, 256), F32), jax.ShapeDtypeStruct((rows, 512), BF16)],
        compiler_params=pltpu.CompilerParams(dimension_semantics=("arbitrary",), vmem_limit_bytes=VMEM_LIMIT),
        name="ssd_conv",
    )(xbc, xbc, xbc, conv_w, conv_b)


def _scan_prepare(g_refs, mbias_ref, dtb_ref, alog_ref, msel_ref, ssel_ref, m_ref, nb,
                  m_exp_ref, m_grow_ref, s_exp_ref, s_grow_ref, s_dtrow_ref):
    bias = mbias_ref[...] + dtb_ref[...]
    gts = [(g_refs[d][...] + bias).T for d in range(N_DIR)]
    a = jnp.concatenate([-jnp.exp(alog_ref[...])] * nb, axis=0)
    n8 = nb * SUBLANES
    lf = [_log_sigmoid(_stack_chunks(gts[d], 8, nb)) for d in range(N_DIR)]
    dt = [_softplus(_stack_chunks(gts[d], 16, nb)) for d in range(N_DIR)]
    csum = _lane_scan_pair(jnp.concatenate([lf[0], dt[0] * a], axis=0), jnp.concatenate([lf[1], dt[1] * a], axis=0),
                           jnp.add, 0.0)
    bcs = [csum[d][0:n8] for d in range(N_DIR)]
    lc = [csum[d][n8:2 * n8] for d in range(N_DIR)]
    g = [_stack_chunks(gts[d], 0, nb) - bcs[d] for d in range(N_DIR)]
    gmax = _lane_scan_pair(g[0], g[1], jnp.maximum, -jnp.inf)
    for d in range(N_DIR):
        last = 0 if d else LANES - 1
        g_last = _lane_bcast(gmax[d], last)
        b_last = _lane_bcast(bcs[d], last)
        lc_last = _lane_bcast(lc[d], last)
        m_prev = m_ref[d]
        m_prevs = [None] * nb
        m_lasts = [None] * nb
        for c in (range(nb - 1, -1, -1) if d else range(nb)):
            r = slice(c * SUBLANES, (c + 1) * SUBLANES)
            m_prevs[c] = m_prev
            m_lasts[c] = jnp.maximum(m_prev, g_last[r])
            m_grow_ref[d, c:c + 1, :] = _head_rows(g[d][r], d)
            s_grow_ref[d, c:c + 1, :] = _head_rows(-lc[d][r], d)
            s_dtrow_ref[d, c:c + 1, :] = _head_rows(dt[d][r], d)
            m_grow_ref[d, NB_MAX + c:NB_MAX + c + 1, 0:256] = _head_scalar_row(jnp.exp(m_prev - m_lasts[c]), d)
            s_grow_ref[d, NB_MAX + c:NB_MAX + c + 1, 0:256] = _head_scalar_row(jnp.exp(lc_last[r]), d)
            m_prev = b_last[r] + m_lasts[c]
        m_ref[d] = m_prev
        mp = jnp.concatenate(m_prevs, axis=0)
        ml = jnp.concatenate(m_lasts, axis=0)
        m_q = jnp.maximum(mp, gmax[d])
        m_quant = (-m_q, jnp.exp(mp - m_q), -(bcs[d] + m_q), jnp.exp(g[d] - ml))
        m_exp_ref[d] = _expand(jnp.concatenate([_unstack_chunks(q, nb) for q in m_quant], axis=0), msel_ref[d])
        s_quant = (lc[d], jnp.exp(lc[d]), jnp.exp(lc_last - lc[d]) * dt[d], jnp.zeros_like(dt[d]))
        s_exp_ref[d] = _expand(jnp.concatenate([_unstack_chunks(q, nb) for q in s_quant], axis=0), ssel_ref[d])


def _ssd_stages(xsf_ref, bcf_ref, xsb_ref, bcb_ref, gf_ref, gb_ref, dtb_ref, sel_ref, alog_ref, dskip_ref, h0_ref,
                yf_ref, yb_ref, hout_ref, h_ref, exp_ref, grow_ref, dtrow_ref, nb):
    lane = _iota((CHUNK, HS), 1)
    row = _iota((CHUNK, HS), 0)
    causal = ((lane % CHUNK) <= row, (lane % CHUNK) >= row)
    r_hs = _iota((HS, 256), 0)
    c_hs = _iota((HS, 256), 1)
    bd_hs = (r_hs // CHUNK) == (c_hs // HD)
    gmask_b = (r_hs // CHUNK // 2) == (c_hs // S_STATE)
    gmask_h = (_iota((256, 256), 0) // S_STATE) == (_iota((256, 256), 1) // HD // 2)
    zero = jnp.zeros((), BF16)

    xs_refs = (xsf_ref, xsb_ref)
    bc_refs = (bcf_ref, bcb_ref)
    y_refs = (yf_ref, yb_ref)

    def init():
        h_ref[...] = h0_ref[0]

    def chunk(i, d):
        c = (nb - 1 - i) if d else i
        rows = slice(c * CHUNK, (c + 1) * CHUNK)
        xs = xs_refs[d][rows, :]
        bm = bc_refs[d][rows, 0:256]
        cm = bc_refs[d][rows, 256:512]
        cb = _dot_nt(cm, jnp.where(gmask_b, _tile4(bm), zero))
        expo = exp_ref[d, rows, 0:HS] + grow_ref[d, c:c + 1, :]
        dmat = jnp.exp(jnp.where(causal[d], expo, -jnp.inf)) * dtrow_ref[d, c:c + 1, :]
        p = (cb * dmat).astype(BF16)
        xbd = jnp.where(bd_hs, _tile4(xs.astype(BF16)), zero)
        y = _dot(p, xbd) + exp_ref[d, rows, HS:HS + 256] * _dot(cm, h_ref[d].astype(BF16))
        if d == 0:
            y = y + dskip_ref[...] * xs
        y_refs[d][rows, :] = y
        xw = (xs * exp_ref[d, rows, HS + 256:HS + 512]).astype(BF16)
        u = _dot_tn(bm, xw)
        da = grow_ref[d, NB_MAX + c:NB_MAX + c + 1, 0:256]
        h_ref[d] = da * h_ref[d] + jnp.where(gmask_h, u, 0.0)

    def final():
        hout_ref[0] = h_ref[...]

    return init, chunk, final


N_M_IN, N_M_OUT, N_M_SCR = 8, 4, 4
N_S_IN, N_S_OUT, N_S_SCR = 11, 3, 4


def _scan_kernel(*refs, nb):
    o = 0
    m_in = refs[o:o + N_M_IN]
    o += N_M_IN
    s_in = refs[o:o + N_S_IN]
    o += N_S_IN
    m_out = refs[o:o + N_M_OUT]
    o += N_M_OUT
    s_out = refs[o:o + N_S_OUT]
    o += N_S_OUT
    m_scr = refs[o:o + N_M_SCR]
    o += N_M_SCR
    s_scr = refs[o:o + N_S_SCR]
    stages = (_mlstm_stages(*m_in, *m_out, *m_scr, nb), _ssd_stages(*s_in, *s_out, *s_scr, nb))
    j = pl.program_id(1)
    nblk = pl.num_programs(1)

    @pl.when(j == 0)
    def _():
        for st in stages:
            st[0]()

    _scan_prepare((m_in[2], m_in[3]), m_in[4], s_in[6], s_in[8], m_in[5], s_in[7], m_scr[1], nb,
                  m_scr[2], m_scr[3], s_scr[1], s_scr[2], s_scr[3])
    for i in range(nb):
        for d in range(N_DIR):
            for st in stages:
                st[1](i, d)

    @pl.when(j == nblk - 1)
    def _():
        for st in stages:
            st[2]()


def _scan_call(qkv, xs, bc, gates, m_bias, m_sel, c0, m0, dtb_row, s_sel, alog8, dskip_exp, h0, batch, seq, rows_blk):
    rows = qkv.shape[0]
    nblk = seq // rows_blk
    nb = rows_blk // CHUNK
    fwd = lambda w: pl.BlockSpec((rows_blk, w), lambda b, j: (b * nblk + j, 0))
    bwd = lambda w: pl.BlockSpec((rows_blk, w), lambda b, j: (b * nblk + nblk - 1 - j, 0))
    per_b = lambda shape: pl.BlockSpec((1,) + shape, lambda b, j: (b,) + tuple(0 for _ in shape))
    const = lambda shape: pl.BlockSpec(shape, lambda b, j: tuple(0 for _ in shape))
    row_out = jax.ShapeDtypeStruct((rows, 256), F32)
    m_in_specs = [fwd(QKV_W), bwd(QKV_W), fwd(LANES), bwd(LANES), const((1, LANES)),
                  const((N_DIR, M_SEL_K, M_EXP_W)), per_b((N_DIR, 256, 512)), per_b((N_DIR, SUBLANES, LANES))]
    s_in_specs = [fwd(256), fwd(512), bwd(256), bwd(512), fwd(LANES), bwd(LANES), const((1, LANES)),
                  const((N_DIR, S_SEL_K, S_EXP_W)), const((SUBLANES, LANES)), const((1, 256)),
                  per_b((N_DIR, 256, 256))]
    m_out_specs = [fwd(256), bwd(256), per_b((N_DIR, 256, LANES)), per_b((N_DIR, SUBLANES, LANES))]
    s_out_specs = [fwd(256), bwd(256), per_b((N_DIR, 256, 256))]
    m_out_shape = [row_out, row_out, jax.ShapeDtypeStruct((batch, N_DIR, 256, LANES), F32),
                   jax.ShapeDtypeStruct((batch, N_DIR, SUBLANES, LANES), F32)]
    s_out_shape = [row_out, row_out, jax.ShapeDtypeStruct((batch, N_DIR, 256, 256), F32)]
    m_scratch = [pltpu.VMEM((N_DIR, 256, 512), F32), pltpu.VMEM((N_DIR, SUBLANES, LANES), F32),
                 pltpu.VMEM((N_DIR, rows_blk, M_EXP_W), F32), pltpu.VMEM((N_DIR, SUBLANES, HS), F32)]
    s_scratch = [pltpu.VMEM((N_DIR, 256, 256), F32), pltpu.VMEM((N_DIR, rows_blk, S_EXP_W), F32),
                 pltpu.VMEM((N_DIR, SUBLANES, HS), F32), pltpu.VMEM((N_DIR, SUBLANES, HS), F32)]
    assert nb <= NB_MAX
    assert (len(m_in_specs), len(m_out_specs), len(m_scratch)) == (N_M_IN, N_M_OUT, N_M_SCR)
    assert (len(s_in_specs), len(s_out_specs), len(s_scratch)) == (N_S_IN, N_S_OUT, N_S_SCR)
    return pl.pallas_call(
        functools.partial(_scan_kernel, nb=nb),
        grid=(batch, nblk),
        in_specs=m_in_specs + s_in_specs,
        out_specs=m_out_specs + s_out_specs,
        out_shape=m_out_shape + s_out_shape,
        scratch_shapes=m_scratch + s_scratch,
        compiler_params=pltpu.CompilerParams(
            dimension_semantics=("arbitrary", "arbitrary"), vmem_limit_bytes=VMEM_LIMIT),
        name="scans",
    )(qkv, qkv, gates, gates, m_bias, m_sel, c0, m0, xs, bc, xs, bc, gates, gates, dtb_row, s_sel, alog8, dskip_exp, h0)


def _mix_kernel(x_ref, hf_ref, hb_ref, misc_ref, pprev_ref, pnext_ref, yf_ref, yb_ref, mod_ref,
                wout_ref, wpool_ref, pscale_ref, mnorm_ref, snorm_ref, wsp_ref, bsp_ref, ln_g_ref, ln_b_ref,
                mean_sel_ref, o_ref, cat_ref, pad_ref, *, seq, per_seq):
    rows = x_ref.shape[0]
    i = pl.program_id(0)
    hs = hf_ref[...] + hb_ref[...]
    mu = _dot_sel2(hs, mean_sel_ref[...])
    hc = hs - mu
    var = _dot_sel2(hc * hc, mean_sel_ref[...])
    y_m = _sigmoid(misc_ref[:, 0:256]) * (hc * lax.rsqrt(var + LN_EPS) * mnorm_ref[...])
    cat_ref[:, 0:256] = y_m.astype(BF16)
    blk = i % per_seq
    xp = misc_ref[:, 256:512]
    pad_ref[0:SUBLANES, :] = jnp.where(blk == 0, 0.0, pprev_ref[...])
    pad_ref[SUBLANES:SUBLANES + rows, :] = xp
    pad_ref[SUBLANES + rows:2 * SUBLANES + rows, :] = jnp.where(blk == per_seq - 1, 0.0, pnext_ref[...])
    n_pad = rows + 2 * SUBLANES
    back = lambda a, k: pltpu.roll(a, k, 0)
    ahead = lambda a, k: pltpu.roll(a, n_pad - k, 0)
    s2 = pad_ref[...]
    s2 = s2 + back(s2, 1)
    s4 = s2 + back(s2, 2)
    lo_half = _iota((n_pad, LANES), 1) < HD
    s4_hi = s4[:, LANES:2 * LANES]
    s8 = s4_hi + back(s4_hi, 4)
    s16 = s8 + back(s8, 8)
    wsum = jnp.concatenate([jnp.where(lo_half, s2[:, 0:LANES], ahead(s4[:, 0:LANES], 1)),
                            jnp.where(lo_half, ahead(s8, 3), ahead(s16, 7))], axis=1)
    wsum = wsum[SUBLANES:SUBLANES + rows, :]
    win = jnp.left_shift(2, _iota((SUBLANES, 256), 1) // HD)
    half = win // 2
    pooled = wsum * (1.0 / win[0:1, :].astype(F32)) - xp

    def edge(r0, t0):
        t = t0 + _iota((SUBLANES, 256), 0)
        cnt = jnp.clip(t - half + win, 0, seq) - jnp.clip(t - half, 0, seq)
        return wsum[r0:r0 + SUBLANES, :] / cnt.astype(F32) - xp[r0:r0 + SUBLANES, :]

    top = jnp.where(blk == 0, edge(0, 0), pooled[0:SUBLANES, :])
    bot = jnp.where(blk == per_seq - 1, edge(rows - SUBLANES, seq - SUBLANES), pooled[rows - SUBLANES:rows, :])
    pooled = jnp.concatenate([top, pooled[SUBLANES:rows - SUBLANES, :], bot], axis=0)
    y_p = _dot(pooled.astype(BF16), wpool_ref[...]) * pscale_ref[...]
    cat_ref[:, 256:512] = y_p.astype(BF16)
    vn = _ln(misc_ref[:, 768:1024]).astype(BF16)
    bd = (_iota((512, 256), 0) // G_CHUNK) == (_iota((512, 256), 1) // HD)
    for c in range(rows // G_CHUNK):
        r = slice(c * G_CHUNK, (c + 1) * G_CHUNK)
        vbd = jnp.where(bd, _tile4(vn[r, :]), jnp.zeros((), BF16))
        mixed = _dot(wsp_ref[...], vbd) + bsp_ref[...]
        cat_ref[r, 512:768] = (misc_ref[r, 512:768] * mixed).astype(BF16)
    y = (yf_ref[...] + yb_ref[...]) * _silu(misc_ref[:, 1024:1280])
    for g in range(2):
        yg = y[:, g * LANES:(g + 1) * LANES]
        yg = yg * lax.rsqrt(jnp.mean(yg * yg, -1, keepdims=True) + LN_EPS)
        cat_ref[:, 768 + g * LANES:768 + (g + 1) * LANES] = (yg * snorm_ref[:, g * LANES:(g + 1) * LANES]).astype(BF16)
    mix = _dot(cat_ref[...], wout_ref[...])
    gate = mod_ref[0, 2:3, :]
    o_ref[...] = _ln(ALPHA * x_ref[...] + gate * mix) * ln_g_ref[...] + ln_b_ref[...]


def _mix_call(x, hf, hb, misc, yf, yb, mod_l, w_out, w_pool_bd, pool_scale, mnorm, snorm, wsp_cat, bsp_exp,
              ln_g, ln_b, mean_sel, seq, rows_blk, cond_of_tile):
    rows = x.shape[0]
    nt = rows // rows_blk
    per_seq = seq // rows_blk
    sub_per_blk = rows_blk // SUBLANES
    n_sub = rows // SUBLANES
    row_spec = lambda w: pl.BlockSpec((rows_blk, w), lambda i: (i, 0))
    const = lambda shape: pl.BlockSpec(shape, lambda i: tuple(0 for _ in shape))
    return pl.pallas_call(
        functools.partial(_mix_kernel, seq=seq, per_seq=per_seq),
        grid=(nt,),
        in_specs=[row_spec(D_MODEL), row_spec(256), row_spec(256), row_spec(MISC_W),
                  pl.BlockSpec((SUBLANES, 256), lambda i: (jnp.maximum(i * sub_per_blk - 1, 0), 1)),
                  pl.BlockSpec((SUBLANES, 256), lambda i: (jnp.minimum((i + 1) * sub_per_blk, n_sub - 1), 1)),
                  row_spec(256), row_spec(256),
                  pl.BlockSpec((1, 6, D_MODEL), lambda i: (cond_of_tile(i), 0, 0)),
                  const((D_MODEL, D_MODEL)), const((256, 256)), const((1, 256)), const((1, 256)), const((1, 256)),
                  const((G_CHUNK, 512)), const((G_CHUNK, 256)), const((1, D_MODEL)), const((1, D_MODEL)),
                  const((256, 256))],
        out_specs=row_spec(D_MODEL),
        out_shape=jax.ShapeDtypeStruct((rows, D_MODEL), F32),
        scratch_shapes=[pltpu.VMEM((rows_blk, D_MODEL), BF16), pltpu.VMEM((rows_blk + 2 * SUBLANES, 256), F32)],
        compiler_params=pltpu.CompilerParams(dimension_semantics=("arbitrary",), vmem_limit_bytes=VMEM_LIMIT),
        name="mix_outproj",
    )(x, hf, hb, misc, misc, misc, yf, yb, mod_l, w_out, w_pool_bd, pool_scale, mnorm, snorm, wsp_cat, bsp_exp,
      ln_g, ln_b, mean_sel)


def _ffn_kernel(x_ref, xprev_ref, xnext_ref, mod_ref, wup_ref, cw_ref, cb_ref, wdown_ref, ln_g_ref, ln_b_ref,
                o_ref, h_ref, act_ref, *, per_seq):
    rows = x_ref.shape[0]
    blk = pl.program_id(0) % per_seq
    shift = mod_ref[0, 3:4, :]
    scale = mod_ref[0, 4:5, :]
    gate = mod_ref[0, 5:6, :]
    modulate = lambda v: (_ln(v) * (1.0 + scale) + shift).astype(BF16)
    x = x_ref[...]
    base = 2 * SUBLANES
    n_pad = rows + 2 * base
    hp = modulate(jnp.concatenate([xprev_ref[...], xprev_ref[...]], axis=0))
    hn = modulate(jnp.concatenate([xnext_ref[...], xnext_ref[...]], axis=0))
    h_ref[0:base, :] = jnp.where(blk > 0, hp, jnp.zeros((), BF16))
    h_ref[base:base + rows, :] = modulate(x)
    h_ref[base + rows:n_pad, :] = jnp.where(blk < per_seq - 1, hn, jnp.zeros((), BF16))
    h = h_ref[...]

    def conv(u, col):
        up = pltpu.roll(u, 1, 0)[base:base + rows, :]
        un = pltpu.roll(u, n_pad - 1, 0)[base:base + rows, :]
        return (cw_ref[0:1, col] * up + cw_ref[1:2, col] * u[base:base + rows, :] + cw_ref[2:3, col] * un
                + cb_ref[:, col])

    for c in range(D_FF // FF_CHUNK):
        cg = slice(c * FF_CHUNK, (c + 1) * FF_CHUNK)
        cv = slice(D_FF + c * FF_CHUNK, D_FF + (c + 1) * FF_CHUNK)
        g = conv(_dot(h, wup_ref[:, cg]), cg)
        val = conv(_dot(h, wup_ref[:, cv]), cv)
        act_ref[:, cg] = (_silu(g) * val).astype(BF16)
    ffn = _dot(act_ref[...], wdown_ref[...])
    o_ref[...] = _ln(ALPHA * x + gate * ffn) * ln_g_ref[...] + ln_b_ref[...]


def _ffn_call(x, mod_l, w_up, conv_w, conv_b, w_down, ln_g, ln_b, seq, rows_blk, cond_of_tile):
    rows = x.shape[0]
    nt = rows // rows_blk
    per_seq = seq // rows_blk
    sub_per_blk = rows_blk // SUBLANES
    n_sub = rows // SUBLANES
    row_spec = lambda w: pl.BlockSpec((rows_blk, w), lambda i: (i, 0))
    const = lambda shape: pl.BlockSpec(shape, lambda i: tuple(0 for _ in shape))
    return pl.pallas_call(
        functools.partial(_ffn_kernel, per_seq=per_seq),
        grid=(nt,),
        in_specs=[row_spec(D_MODEL),
                  pl.BlockSpec((SUBLANES, D_MODEL), lambda i: (jnp.maximum(i * sub_per_blk - 1, 0), 0)),
                  pl.BlockSpec((SUBLANES, D_MODEL), lambda i: (jnp.minimum((i + 1) * sub_per_blk, n_sub - 1), 0)),
                  pl.BlockSpec((1, 6, D_MODEL), lambda i: (cond_of_tile(i), 0, 0)),
                  const((D_MODEL, 2 * D_FF)), const((3, 2 * D_FF)), const((1, 2 * D_FF)),
                  const((D_FF, D_MODEL)), const((1, D_MODEL)), const((1, D_MODEL))],
        out_specs=row_spec(D_MODEL),
        out_shape=jax.ShapeDtypeStruct((rows, D_MODEL), F32),
        scratch_shapes=[pltpu.VMEM((rows_blk + 4 * SUBLANES, D_MODEL), BF16), pltpu.VMEM((rows_blk, D_FF), BF16)],
        compiler_params=pltpu.CompilerParams(dimension_semantics=("arbitrary",), vmem_limit_bytes=VMEM_LIMIT),
        name="conv_ffn",
    )(x, x, x, mod_l, w_up, conv_w, conv_b, w_down, ln_g, ln_b)


def _grid_pos_embed(rows, dim):
    quarter = dim // 4
    freq = 1.0 / (POS_BASE ** (jnp.arange(quarter, dtype=F32) / quarter))
    r = jnp.repeat(jnp.arange(rows, dtype=F32), GRID_W)
    col = jnp.tile(jnp.arange(GRID_W, dtype=F32), rows)

    def enc(pos):
        ang = pos[:, None] * freq[None, :]
        return jnp.concatenate([jnp.sin(ang), jnp.cos(ang)], -1)

    return jnp.concatenate([enc(r), enc(col)], -1)


def _selector(n_quant, pad_rows, widths):
    sel = np.zeros((N_DIR, 3 * pad_rows, sum(widths)), np.float32)
    offs = np.concatenate([[0], np.cumsum(widths)])
    for d in range(N_DIR):
        for p in range(3):
            for q in range(n_quant):
                per_head = widths[q] // HEADS
                for h in range(HEADS):
                    lo = offs[q] + h * per_head
                    sel[d, p * pad_rows + q * SUBLANES + d * HEADS + h, lo:lo + per_head] = 1.0
    return jnp.asarray(sel, BF16)


def _constants():
    m_sel = _selector(4, 32, [HS, 256, 256, 256])
    s_sel = _selector(3, 32, [HS, 256, 256])
    mean_sel = np.kron(np.eye(HEADS, dtype=np.float32), np.full((HD, HD), 1.0 / HD, np.float32))
    return m_sel, s_sel, jnp.asarray(mean_sel, BF16)


def _permute_w_in(w):
    pad = jnp.zeros((D_MODEL, LANES - 24), w.dtype)
    return jnp.concatenate([w[:, 0:768], w[:, 768:1024], w[:, 1040:1296], w[:, 1296:1808], w[:, 1808:2064],
                            w[:, 2064:2832], w[:, 1024:1040], w[:, 2832:2840], pad], axis=1).astype(BF16)


def _block_diag4(blocks):
    a, b = blocks.shape[1:]
    eye = jnp.eye(HEADS, dtype=blocks.dtype)
    return (eye[:, None, :, None] * blocks[:, :, None, :]).reshape(HEADS * a, HEADS * b)


def _dir_head_rows(v):
    rep = jnp.broadcast_to(v[..., None], v.shape + (LANES,))
    zeros = jnp.zeros_like(rep[..., 0, :, :])
    return jnp.stack([jnp.concatenate([rep[..., 0, :, :], zeros], axis=-2),
                      jnp.concatenate([zeros, rep[..., 1, :, :]], axis=-2)], axis=-3)


def _pack_mlstm_state(c, n, m):
    eye = jnp.eye(HEADS, dtype=F32)
    cbd = (eye[None, None, :, None, :, None] * c[:, :, :, :, None, :]).reshape(c.shape[0], N_DIR, 256, 256)
    nbd = jnp.broadcast_to((eye[None, None, :, None, :, None] * n[:, :, :, :, None, None]),
                           (c.shape[0], N_DIR, HEADS, HD, HEADS, HD)).reshape(c.shape[0], N_DIR, 256, 256)
    return jnp.concatenate([cbd, nbd], axis=-1), _dir_head_rows(m)


def _unpack_mlstm_state(cout, mout):
    b = cout.shape[0]
    c = cout[..., 0:HD].reshape(b, N_DIR, HEADS, HD, HD)
    n = cout[..., HD].reshape(b, N_DIR, HEADS, HD)
    m = jnp.stack([mout[:, d, d * HEADS:(d + 1) * HEADS, 0] for d in range(N_DIR)], axis=1)
    return c, n, m


def _pack_ssd_state(s):
    b = s.shape[0]
    st = jnp.transpose(s, (0, 1, 4, 2, 3))
    grp = (jnp.arange(HEADS) // 2)[None, :] == jnp.arange(2)[:, None]
    full = st[:, :, None, :, :, :] * grp[None, None, :, None, :, None].astype(F32)
    return full.reshape(b, N_DIR, 256, 256)


def _unpack_ssd_state(hp):
    heads = [jnp.swapaxes(hp[:, :, (h // 2) * S_STATE:(h // 2 + 1) * S_STATE, h * HD:(h + 1) * HD], -1, -2)
             for h in range(HEADS)]
    return jnp.stack(heads, axis=2)


def _layer(x, pos, mod_l, p, m_state, s_state, batch, seq, rows_blk, cond_of_tile, consts):
    m_sel, s_sel, mean_sel = consts
    outs = _inproj_call(x, pos, mod_l, p["w_in"], seq, rows_blk, cond_of_tile)
    if pos is not None:
        qkv, misc, xbc, gates, x = outs
    else:
        qkv, misc, xbc, gates = outs
    xs, bc = _ssd_conv_call(xbc, p["ssd_conv_w"], p["ssd_conv_b"], seq, rows_blk)
    hf, hb, cout, mout, yf, yb, hs = _scan_call(qkv, xs, bc, gates, p["m_bias"], m_sel, m_state[0], m_state[1],
                                                p["dt_bias"], s_sel, p["a_log"], p["d_skip"], s_state, batch, seq,
                                                rows_blk)
    x1 = _mix_call(x, hf, hb, misc, yf, yb, mod_l, p["w_out"], p["w_pool"], p["pool_scale"], p["mnorm"],
                   p["snorm"], p["wsp"], p["bsp"], p["ln1_g"], p["ln1_b"], mean_sel, seq, rows_blk, cond_of_tile)
    x2 = _ffn_call(x1, mod_l, p["ffn_w_up"], p["ffn_conv_w"], p["ffn_conv_b"], p["ffn_w_down"], p["ln2_g"],
                   p["ln2_b"], seq, rows_blk, cond_of_tile)
    return x2, (cout, mout), hs


def kernel(x_prompt, x_sample, state_mlstm_c, state_mlstm_n, state_mlstm_m, state_ssd, c, c_ctx, w_ada, b_ada, w_in, b_igate, b_fgate, mlstm_norm_g, w_pool, pool_scale, w_spatial, b_spatial, ssd_conv_w, ssd_conv_b, ssd_dt_bias, ssd_a_log, ssd_d, ssd_norm_g, w_out, ln1_g, ln1_b, ffn_w_up, ffn_conv_w, ffn_conv_b, ffn_w_down, ln2_g, ln2_b):
    n_ctx, t_ctx, _ = x_prompt.shape
    n_lat, t_lat, _ = x_sample.shape
    consts = _constants()
    pos = _grid_pos_embed(t_lat // GRID_W, D_MODEL)

    cond = jnp.zeros((N_COND, D_MODEL), F32).at[0:n_lat].set(c).at[CTX_COND].set(c_ctx)
    mod = _ada_call(cond, w_ada, b_ada).reshape(DEPTH, N_COND, 6, D_MODEL)

    ctx_blk = min(t_ctx, 256)
    lat_blk = min(t_lat, 512)
    lat_per_seq = t_lat // lat_blk
    ctx_cond = lambda i: CTX_COND
    lat_cond = lambda i: i // lat_per_seq

    y_p = x_prompt.reshape(n_ctx * t_ctx, D_MODEL)
    y_s = x_sample.reshape(n_lat * t_lat, D_MODEL)
    zero_m = (jnp.zeros((n_ctx, N_DIR, 256, 512), F32), jnp.zeros((n_ctx, N_DIR, SUBLANES, LANES), F32))
    zero_s = jnp.zeros((n_ctx, N_DIR, 256, 256), F32)
    new_c, new_n, new_m, new_s = [], [], [], []
    for l in range(DEPTH):
        m_bias = jnp.zeros((1, LANES), F32).at[0, 0:8].set(b_igate[l].reshape(-1)).at[0, 8:16].set(b_fgate[l].reshape(-1))
        dt_bias = jnp.zeros((1, LANES), F32).at[0, 16:24].set(ssd_dt_bias[l].reshape(-1))
        p = {
            "w_in": _permute_w_in(w_in[l]),
            "m_bias": m_bias,
            "dt_bias": dt_bias,
            "a_log": jnp.broadcast_to(ssd_a_log[l].reshape(N_DIR * HEADS, 1), (N_DIR * HEADS, LANES)),
            "d_skip": jnp.repeat(ssd_d[l], HD)[None, :],
            "ssd_conv_w": ssd_conv_w[l],
            "ssd_conv_b": ssd_conv_b[l][None, :],
            "w_out": w_out[l].astype(BF16),
            "w_pool": _block_diag4(w_pool[l]).astype(BF16),
            "pool_scale": pool_scale[l][None, :],
            "mnorm": mlstm_norm_g[l][None, :],
            "snorm": ssd_norm_g[l][None, :],
            "wsp": jnp.transpose(w_spatial[l], (1, 0, 2)).reshape(G_CHUNK, HEADS * G_CHUNK).astype(BF16),
            "bsp": jnp.repeat(b_spatial[l].T, HD, axis=-1),
            "ln1_g": ln1_g[l][None, :], "ln1_b": ln1_b[l][None, :],
            "ffn_w_up": ffn_w_up[l].astype(BF16),
            "ffn_conv_w": ffn_conv_w[l],
            "ffn_conv_b": ffn_conv_b[l][None, :],
            "ffn_w_down": ffn_w_down[l].astype(BF16),
            "ln2_g": ln2_g[l][None, :], "ln2_b": ln2_b[l][None, :],
        }
        y_p, (cout, mout), hs = _layer(y_p, None, mod[l], p, zero_m, zero_s, n_ctx, t_ctx, ctx_blk, ctx_cond, consts)
        cc, nn, mm = _unpack_mlstm_state(cout, mout)
        new_c.append(cc)
        new_n.append(nn)
        new_m.append(mm)
        new_s.append(_unpack_ssd_state(hs))
        lat_m = _pack_mlstm_state(state_mlstm_c[:, l], state_mlstm_n[:, l], state_mlstm_m[:, l])
        y_s, _, _ = _layer(y_s, pos if l == 0 else None, mod[l], p, lat_m, _pack_ssd_state(state_ssd[:, l]),
                           n_lat, t_lat, lat_blk, lat_cond, consts)
    return (y_p.reshape(n_ctx, t_ctx, D_MODEL), y_s.reshape(n_lat, t_lat, D_MODEL),
            jnp.stack(new_c, 1), jnp.stack(new_n, 1), jnp.stack(new_m, 1), jnp.stack(new_s, 1))
```

```python
import functools

import numpy as np
import jax
import jax.numpy as jnp
from jax import lax
from jax.experimental import pallas as pl
from jax.experimental.pallas import tpu as pltpu

F32 = jnp.float32
BF16 = jnp.bfloat16

D_MODEL = 1024
DEPTH = 2
GRID_W = 64
POS_BASE = 10000.0
W_GROUP = 256
N_DIR = 2
HEADS = 4
HD = 64
G_CHUNK = 128
S_STATE = 128
S_XBC = 768
D_FF = 2816
FF_CHUNK = 256
ALPHA = (2 * DEPTH) ** 0.25
LN_EPS = 1e-5
N_COND = 16
CTX_COND = 8
LANES = 128
SUBLANES = 8
CHUNK = LANES
HS = HEADS * CHUNK
SCAN_RADIX = 8
QKV_W = 768
MISC_W = 1280
IN_W = QKV_W + MISC_W + S_XBC + LANES
M_EXP_W = HS + 3 * W_GROUP
S_EXP_W = HS + 2 * W_GROUP
M_SEL_K = 3 * 32
S_SEL_K = 3 * 32
NB_MAX = 4
VMEM_LIMIT = 56 * 1024 * 1024


def _dot(a, b):
    return jnp.dot(a, b, preferred_element_type=F32)


def _dot_nt(a, b):
    return lax.dot_general(a, b, (((1,), (1,)), ((), ())), preferred_element_type=F32)


def _dot_tn(a, b):
    return lax.dot_general(a, b, (((0,), (0,)), ((), ())), preferred_element_type=F32)


def _split3(x):
    hi = x.astype(BF16)
    r1 = x - hi.astype(F32)
    mid = r1.astype(BF16)
    lo = (r1 - mid.astype(F32)).astype(BF16)
    return hi, mid, lo


def _dot_sel2(x, sel):
    hi = x.astype(BF16)
    lo = (x - hi.astype(F32)).astype(BF16)
    return _dot(hi, sel) + _dot(lo, sel)


def _expand(rows_f32, sel):
    hi, mid, lo = _split3(rows_f32)
    return _dot_tn(jnp.concatenate([hi, mid, lo], axis=0), sel)


def _ln(x):
    mu = jnp.mean(x, -1, keepdims=True)
    xc = x - mu
    var = jnp.mean(xc * xc, -1, keepdims=True)
    return xc * lax.rsqrt(var + LN_EPS)


def _sigmoid(x):
    return 1.0 / (1.0 + jnp.exp(-x))


def _silu(x):
    return x * _sigmoid(x)


def _softplus(x):
    return jnp.maximum(x, 0.0) + jnp.log1p(jnp.exp(-jnp.abs(x)))


def _log_sigmoid(x):
    return jnp.minimum(x, 0.0) - jnp.log1p(jnp.exp(-jnp.abs(x)))


def _iota(shape, dim):
    return lax.broadcasted_iota(jnp.int32, shape, dim)


def _lane_scan_pair(xf, xr, op, ident):
    pos_f = _iota(xf.shape, 1)
    pos_r = _iota(xr.shape, 1)
    k = 1
    while k < LANES:
        acc_f, acc_r = xf, xr
        for mult in range(1, SCAN_RADIX):
            s = k * mult
            if s >= LANES:
                break
            acc_f = op(acc_f, jnp.where(pos_f >= s, pltpu.roll(xf, s, 1), ident))
            acc_r = op(acc_r, jnp.where(pos_r < LANES - s, pltpu.roll(xr, LANES - s, 1), ident))
        xf, xr = acc_f, acc_r
        k *= SCAN_RADIX
    return xf, xr


def _lane_bcast(x, lane):
    return jnp.broadcast_to(x[:, lane:lane + 1], x.shape)


def _stack_chunks(gt, r0, nb):
    return jnp.concatenate([gt[r0:r0 + SUBLANES, c * CHUNK:(c + 1) * CHUNK] for c in range(nb)], axis=0)


def _unstack_chunks(x, nb):
    return jnp.concatenate([x[c * SUBLANES:(c + 1) * SUBLANES] for c in range(nb)], axis=1)


def _head_rows(x, d):
    return jnp.concatenate([x[d * HEADS + h:d * HEADS + h + 1, :] for h in range(HEADS)], axis=1)


def _head_scalar_row(x, d):
    lo = _iota((1, LANES), 1) < HD
    r = d * HEADS
    return jnp.concatenate([jnp.where(lo, x[r:r + 1, :], x[r + 1:r + 2, :]),
                            jnp.where(lo, x[r + 2:r + 3, :], x[r + 3:r + 4, :])], axis=1)


def _tile4(x):
    return jnp.concatenate([x, x, x, x], axis=0)


def _ada_kernel(cond_ref, w_ref, b_ref, o_ref):
    s = _silu(cond_ref[...]).astype(BF16)
    o_ref[0] = _dot(s, w_ref[0].astype(BF16)) + b_ref[0]


def _ada_call(cond, w_ada, b_ada):
    n_out = w_ada.shape[-1]
    tn = 1536
    return pl.pallas_call(
        _ada_kernel,
        grid=(DEPTH, n_out // tn),
        in_specs=[
            pl.BlockSpec((N_COND, D_MODEL), lambda l, n: (0, 0)),
            pl.BlockSpec((1, D_MODEL, tn), lambda l, n: (l, 0, n)),
            pl.BlockSpec((1, 1, tn), lambda l, n: (l, 0, n)),
        ],
        out_specs=pl.BlockSpec((1, N_COND, tn), lambda l, n: (l, 0, n)),
        out_shape=jax.ShapeDtypeStruct((DEPTH, N_COND, n_out), F32),
        compiler_params=pltpu.CompilerParams(
            dimension_semantics=("arbitrary", "arbitrary"), vmem_limit_bytes=VMEM_LIMIT),
        name="ada_mod",
    )(cond, w_ada, b_ada.reshape(DEPTH, 1, n_out))


def _inproj_kernel(*refs, has_pos, per_seq):
    if has_pos:
        (x_ref, xp_ref, xn_ref, pos_ref, pp_ref, pn_ref, mod_ref, w_ref, cw_ref, cb_ref,
         qkv_ref, misc_ref, xs_ref, bc_ref, gates_ref, xres_ref, hx_ref) = refs
        x = x_ref[...] + pos_ref[...]
        x_prev = xp_ref[...] + pp_ref[...]
        x_next = xn_ref[...] + pn_ref[...]
        xres_ref[...] = x
    else:
        (x_ref, xp_ref, xn_ref, mod_ref, w_ref, cw_ref, cb_ref,
         qkv_ref, misc_ref, xs_ref, bc_ref, gates_ref, hx_ref) = refs
        x = x_ref[...]
        x_prev = xp_ref[...]
        x_next = xn_ref[...]
    rows = x.shape[0]
    blk = pl.program_id(0) % per_seq
    shift = mod_ref[0, 0:1, :]
    scale = mod_ref[0, 1:2, :]
    modulate = lambda v: (_ln(v) * (1.0 + scale) + shift).astype(BF16)
    base = 2 * SUBLANES
    n_pad = rows + 2 * base
    hx_ref[0:base, :] = jnp.where(blk > 0, modulate(jnp.concatenate([x_prev, x_prev], axis=0)), jnp.zeros((), BF16))
    hx_ref[base:base + rows, :] = modulate(x)
    hx_ref[base + rows:n_pad, :] = jnp.where(blk < per_seq - 1, modulate(jnp.concatenate([x_next, x_next], axis=0)),
                                             jnp.zeros((), BF16))
    u = _dot(hx_ref[...], w_ref[:, QKV_W + MISC_W:QKV_W + MISC_W + S_XBC])
    act = _silu(cw_ref[0:1, :] * pltpu.roll(u, 1, 0)[base:base + rows, :] + cw_ref[1:2, :] * u[base:base + rows, :]
                + cw_ref[2:3, :] * pltpu.roll(u, n_pad - 1, 0)[base:base + rows, :] + cb_ref[...])
    xs_ref[...] = act[:, 0:256]
    bc_ref[...] = act[:, 256:768].astype(BF16)
    h = hx_ref[base:base + rows, :]
    qkv = _dot(h, w_ref[:, 0:QKV_W])
    qkv_ref[:, 0:256] = qkv[:, 0:256].astype(BF16)
    qkv_ref[:, 256:512] = (qkv[:, 256:512] * (HD ** -0.5)).astype(BF16)
    qkv_ref[:, 512:768] = qkv[:, 512:768].astype(BF16)
    misc_ref[...] = _dot(h, w_ref[:, QKV_W:QKV_W + MISC_W])
    gates_ref[...] = _dot(h, w_ref[:, QKV_W + MISC_W + S_XBC:IN_W])


def _inproj_call(x, pos, mod_l, w_in_p, conv_w, conv_b, seq, rows_blk, cond_of_tile):
    rows = x.shape[0]
    nt = rows // rows_blk
    per_seq = seq // rows_blk
    sub_per_blk = rows_blk // SUBLANES
    has_pos = pos is not None
    row_spec = lambda w: pl.BlockSpec((rows_blk, w), lambda i: (i, 0))
    const = lambda shape: pl.BlockSpec(shape, lambda i: tuple(0 for _ in shape))
    halo = lambda n_sub, blk_of, step: pl.BlockSpec(
        (SUBLANES, D_MODEL), lambda i: (jnp.clip(blk_of(i) * sub_per_blk + step, 0, n_sub - 1), 0))
    in_specs = [row_spec(D_MODEL), halo(rows // SUBLANES, lambda i: i, -1),
                halo(rows // SUBLANES, lambda i: i, sub_per_blk)]
    args = [x, x, x]
    if has_pos:
        in_specs += [pl.BlockSpec((rows_blk, D_MODEL), lambda i: (i % per_seq, 0)),
                     halo(seq // SUBLANES, lambda i: i % per_seq, -1),
                     halo(seq // SUBLANES, lambda i: i % per_seq, sub_per_blk)]
        args += [pos, pos, pos]
    in_specs += [pl.BlockSpec((1, 6, D_MODEL), lambda i: (cond_of_tile(i), 0, 0)), const((D_MODEL, IN_W)),
                 const((3, S_XBC)), const((1, S_XBC))]
    args += [mod_l, w_in_p, conv_w, conv_b]
    out_specs = [row_spec(QKV_W), row_spec(MISC_W), row_spec(256), row_spec(512), row_spec(LANES)]
    out_shape = [jax.ShapeDtypeStruct((rows, QKV_W), BF16), jax.ShapeDtypeStruct((rows, MISC_W), F32),
                 jax.ShapeDtypeStruct((rows, 256), F32), jax.ShapeDtypeStruct((rows, 512), BF16),
                 jax.ShapeDtypeStruct((rows, LANES), F32)]
    if has_pos:
        out_specs.append(row_spec(D_MODEL))
        out_shape.append(jax.ShapeDtypeStruct((rows, D_MODEL), F32))
    return pl.pallas_call(
        functools.partial(_inproj_kernel, has_pos=has_pos, per_seq=per_seq),
        grid=(nt,), in_specs=in_specs, out_specs=out_specs, out_shape=out_shape,
        scratch_shapes=[pltpu.VMEM((rows_blk + 4 * SUBLANES, D_MODEL), BF16)],
        compiler_params=pltpu.CompilerParams(dimension_semantics=("arbitrary",), vmem_limit_bytes=VMEM_LIMIT),
        name="inproj",
    )(*args)


def _mlstm_stages(qkvf_ref, qkvb_ref, gf_ref, gb_ref, bias_ref, sel_ref, c0_ref, m0_ref,
                  hf_ref, hb_ref, cout_ref, mout_ref, caug_ref, m_ref, exp_ref, grow_ref, nb):
    lane = _iota((CHUNK, HS), 1)
    row = _iota((CHUNK, HS), 0)
    causal = ((lane % CHUNK) <= row, (lane % CHUNK) >= row)
    bd_hs = (_iota((HS, 256), 0) // CHUNK) == (_iota((HS, 256), 1) // HD)
    bd = (_iota((256, 256), 0) // HD) == (_iota((256, 256), 1) // HD)
    bd2 = jnp.concatenate([bd, bd], axis=1)
    ones_rows = jnp.ones((CHUNK, 256), BF16)
    zero = jnp.zeros((), BF16)

    qkv_refs = (qkvf_ref, qkvb_ref)
    h_refs = (hf_ref, hb_ref)

    def init():
        caug_ref[...] = c0_ref[0]
        m_ref[...] = m0_ref[0]

    def chunk(i, d):
        c = (nb - 1 - i) if d else i
        rows = slice(c * CHUNK, (c + 1) * CHUNK)
        q = qkv_refs[d][rows, 0:256]
        k = qkv_refs[d][rows, 256:512]
        v = qkv_refs[d][rows, 512:768]
        s = _dot_nt(q, jnp.where(bd_hs, _tile4(k), zero))
        expo = exp_ref[d, rows, 0:HS] + grow_ref[d, c:c + 1, :]
        p = s * jnp.exp(jnp.where(causal[d], expo, -jnp.inf))
        den_intra = jnp.concatenate(
            [jnp.broadcast_to(jnp.sum(p[:, h * CHUNK:(h + 1) * CHUNK], axis=-1, keepdims=True), (CHUNK, HD))
             for h in range(HEADS)], axis=1)
        num_intra = _dot(p.astype(BF16), jnp.where(bd_hs, _tile4(v), zero))
        qc = _dot(q, caug_ref[d].astype(BF16))
        ew = exp_ref[d, rows, HS:HS + 256]
        num = num_intra + ew * qc[:, 0:256]
        den = den_intra + ew * qc[:, 256:512]
        h_refs[d][rows, :] = num / jnp.maximum(jnp.abs(den), jnp.exp(exp_ref[d, rows, HS + 256:HS + 512]))
        kw = (k.astype(F32) * exp_ref[d, rows, HS + 512:HS + 768]).astype(BF16)
        u = _dot_tn(kw, jnp.concatenate([v, ones_rows], axis=1))
        wc = grow_ref[d, NB_MAX + c:NB_MAX + c + 1, 0:256]
        caug_ref[d] = jnp.concatenate([wc, wc], axis=1) * caug_ref[d] + jnp.where(bd2, u, 0.0)

    def final():
        mout_ref[0] = m_ref[...]
        for d in range(N_DIR):
            for h in range(HEADS):
                r = slice(h * HD, (h + 1) * HD)
                cout_ref[0, d, r, 0:HD] = caug_ref[d, r, h * HD:(h + 1) * HD]
                cout_ref[0, d, r, HD:2 * HD] = caug_ref[d, r, 256 + h * HD:256 + (h + 1) * HD]

    return init, chunk, final


def _scan_prepare(g_refs, mbias_ref, dtb_ref, alog_ref, msel_ref, ssel_ref, m_ref, nb,
                  m_exp_ref, m_grow_ref, s_exp_ref, s_grow_ref, s_dtrow_ref):
    bias = mbias_ref[...] + dtb_ref[...]
    gts = [(g_refs[d][...] + bias).T for d in range(N_DIR)]
    a = jnp.concatenate([-jnp.exp(alog_ref[...])] * nb, axis=0)
    n8 = nb * SUBLANES
    lf = [_log_sigmoid(_stack_chunks(gts[d], 8, nb)) for d in range(N_DIR)]
    dt = [_softplus(_stack_chunks(gts[d], 16, nb)) for d in range(N_DIR)]
    csum = _lane_scan_pair(jnp.concatenate([lf[0], dt[0] * a], axis=0), jnp.concatenate([lf[1], dt[1] * a], axis=0),
                           jnp.add, 0.0)
    bcs = [csum[d][0:n8] for d in range(N_DIR)]
    lc = [csum[d][n8:2 * n8] for d in range(N_DIR)]
    g = [_stack_chunks(gts[d], 0, nb) - bcs[d] for d in range(N_DIR)]
    gmax = _lane_scan_pair(g[0], g[1], jnp.maximum, -jnp.inf)
    for d in range(N_DIR):
        last = 0 if d else LANES - 1
        g_last = _lane_bcast(gmax[d], last)
        b_last = _lane_bcast(bcs[d], last)
        lc_last = _lane_bcast(lc[d], last)
        m_prev = m_ref[d]
        m_prevs = [None] * nb
        m_lasts = [None] * nb
        for c in (range(nb - 1, -1, -1) if d else range(nb)):
            r = slice(c * SUBLANES, (c + 1) * SUBLANES)
            m_prevs[c] = m_prev
            m_lasts[c] = jnp.maximum(m_prev, g_last[r])
            m_grow_ref[d, c:c + 1, :] = _head_rows(g[d][r], d)
            s_grow_ref[d, c:c + 1, :] = _head_rows(-lc[d][r], d)
            s_dtrow_ref[d, c:c + 1, :] = _head_rows(dt[d][r], d)
            m_grow_ref[d, NB_MAX + c:NB_MAX + c + 1, 0:256] = _head_scalar_row(jnp.exp(m_prev - m_lasts[c]), d)
            s_grow_ref[d, NB_MAX + c:NB_MAX + c + 1, 0:256] = _head_scalar_row(jnp.exp(lc_last[r]), d)
            m_prev = b_last[r] + m_lasts[c]
        m_ref[d] = m_prev
        mp = jnp.concatenate(m_prevs, axis=0)
        ml = jnp.concatenate(m_lasts, axis=0)
        m_q = jnp.maximum(mp, gmax[d])
        m_quant = (-m_q, jnp.exp(mp - m_q), -(bcs[d] + m_q), jnp.exp(g[d] - ml))
        m_exp_ref[d] = _expand(jnp.concatenate([_unstack_chunks(q, nb) for q in m_quant], axis=0), msel_ref[d])
        s_quant = (lc[d], jnp.exp(lc[d]), jnp.exp(lc_last - lc[d]) * dt[d], jnp.zeros_like(dt[d]))
        s_exp_ref[d] = _expand(jnp.concatenate([_unstack_chunks(q, nb) for q in s_quant], axis=0), ssel_ref[d])


def _ssd_stages(xsf_ref, bcf_ref, xsb_ref, bcb_ref, gf_ref, gb_ref, dtb_ref, sel_ref, alog_ref, dskip_ref, h0_ref,
                yf_ref, yb_ref, hout_ref, h_ref, exp_ref, grow_ref, dtrow_ref, nb):
    lane = _iota((CHUNK, HS), 1)
    row = _iota((CHUNK, HS), 0)
    causal = ((lane % CHUNK) <= row, (lane % CHUNK) >= row)
    r_hs = _iota((HS, 256), 0)
    c_hs = _iota((HS, 256), 1)
    bd_hs = (r_hs // CHUNK) == (c_hs // HD)
    gmask_b = (r_hs // CHUNK // 2) == (c_hs // S_STATE)
    gmask_h = (_iota((256, 256), 0) // S_STATE) == (_iota((256, 256), 1) // HD // 2)
    zero = jnp.zeros((), BF16)

    xs_refs = (xsf_ref, xsb_ref)
    bc_refs = (bcf_ref, bcb_ref)
    y_refs = (yf_ref, yb_ref)

    def init():
        h_ref[...] = h0_ref[0]

    def chunk(i, d):
        c = (nb - 1 - i) if d else i
        rows = slice(c * CHUNK, (c + 1) * CHUNK)
        xs = xs_refs[d][rows, :]
        bm = bc_refs[d][rows, 0:256]
        cm = bc_refs[d][rows, 256:512]
        cb = _dot_nt(cm, jnp.where(gmask_b, _tile4(bm), zero))
        expo = exp_ref[d, rows, 0:HS] + grow_ref[d, c:c + 1, :]
        dmat = jnp.exp(jnp.where(causal[d], expo, -jnp.inf)) * dtrow_ref[d, c:c + 1, :]
        p = (cb * dmat).astype(BF16)
        xbd = jnp.where(bd_hs, _tile4(xs.astype(BF16)), zero)
        y = _dot(p, xbd) + exp_ref[d, rows, HS:HS + 256] * _dot(cm, h_ref[d].astype(BF16))
        if d == 0:
            y = y + dskip_ref[...] * xs
        y_refs[d][rows, :] = y
        xw = (xs * exp_ref[d, rows, HS + 256:HS + 512]).astype(BF16)
        u = _dot_tn(bm, xw)
        da = grow_ref[d, NB_MAX + c:NB_MAX + c + 1, 0:256]
        h_ref[d] = da * h_ref[d] + jnp.where(gmask_h, u, 0.0)

    def final():
        hout_ref[0] = h_ref[...]

    return init, chunk, final


N_M_IN, N_M_OUT, N_M_SCR = 8, 4, 4
N_S_IN, N_S_OUT, N_S_SCR = 11, 3, 4


def _scan_kernel(*refs, nb):
    o = 0
    m_in = refs[o:o + N_M_IN]
    o += N_M_IN
    s_in = refs[o:o + N_S_IN]
    o += N_S_IN
    m_out = refs[o:o + N_M_OUT]
    o += N_M_OUT
    s_out = refs[o:o + N_S_OUT]
    o += N_S_OUT
    m_scr = refs[o:o + N_M_SCR]
    o += N_M_SCR
    s_scr = refs[o:o + N_S_SCR]
    stages = (_mlstm_stages(*m_in, *m_out, *m_scr, nb), _ssd_stages(*s_in, *s_out, *s_scr, nb))
    j = pl.program_id(1)
    nblk = pl.num_programs(1)

    @pl.when(j == 0)
    def _():
        for st in stages:
            st[0]()

    _scan_prepare((m_in[2], m_in[3]), m_in[4], s_in[6], s_in[8], m_in[5], s_in[7], m_scr[1], nb,
                  m_scr[2], m_scr[3], s_scr[1], s_scr[2], s_scr[3])
    for i in range(nb):
        for d in range(N_DIR):
            for st in stages:
                st[1](i, d)

    @pl.when(j == nblk - 1)
    def _():
        for st in stages:
            st[2]()


def _scan_call(qkv, xs, bc, gates, m_bias, m_sel, c0, m0, dtb_row, s_sel, alog8, dskip_exp, h0, batch, seq, rows_blk):
    rows = qkv.shape[0]
    nblk = seq // rows_blk
    nb = rows_blk // CHUNK
    fwd = lambda w: pl.BlockSpec((rows_blk, w), lambda b, j: (b * nblk + j, 0))
    bwd = lambda w: pl.BlockSpec((rows_blk, w), lambda b, j: (b * nblk + nblk - 1 - j, 0))
    per_b = lambda shape: pl.BlockSpec((1,) + shape, lambda b, j: (b,) + tuple(0 for _ in shape))
    const = lambda shape: pl.BlockSpec(shape, lambda b, j: tuple(0 for _ in shape))
    row_out = jax.ShapeDtypeStruct((rows, 256), F32)
    m_in_specs = [fwd(QKV_W), bwd(QKV_W), fwd(LANES), bwd(LANES), const((1, LANES)),
                  const((N_DIR, M_SEL_K, M_EXP_W)), per_b((N_DIR, 256, 512)), per_b((N_DIR, SUBLANES, LANES))]
    s_in_specs = [fwd(256), fwd(512), bwd(256), bwd(512), fwd(LANES), bwd(LANES), const((1, LANES)),
                  const((N_DIR, S_SEL_K, S_EXP_W)), const((SUBLANES, LANES)), const((1, 256)),
                  per_b((N_DIR, 256, 256))]
    m_out_specs = [fwd(256), bwd(256), per_b((N_DIR, 256, LANES)), per_b((N_DIR, SUBLANES, LANES))]
    s_out_specs = [fwd(256), bwd(256), per_b((N_DIR, 256, 256))]
    m_out_shape = [row_out, row_out, jax.ShapeDtypeStruct((batch, N_DIR, 256, LANES), F32),
                   jax.ShapeDtypeStruct((batch, N_DIR, SUBLANES, LANES), F32)]
    s_out_shape = [row_out, row_out, jax.ShapeDtypeStruct((batch, N_DIR, 256, 256), F32)]
    m_scratch = [pltpu.VMEM((N_DIR, 256, 512), F32), pltpu.VMEM((N_DIR, SUBLANES, LANES), F32),
                 pltpu.VMEM((N_DIR, rows_blk, M_EXP_W), F32), pltpu.VMEM((N_DIR, SUBLANES, HS), F32)]
    s_scratch = [pltpu.VMEM((N_DIR, 256, 256), F32), pltpu.VMEM((N_DIR, rows_blk, S_EXP_W), F32),
                 pltpu.VMEM((N_DIR, SUBLANES, HS), F32), pltpu.VMEM((N_DIR, SUBLANES, HS), F32)]
    assert nb <= NB_MAX
    assert (len(m_in_specs), len(m_out_specs), len(m_scratch)) == (N_M_IN, N_M_OUT, N_M_SCR)
    assert (len(s_in_specs), len(s_out_specs), len(s_scratch)) == (N_S_IN, N_S_OUT, N_S_SCR)
    return pl.pallas_call(
        functools.partial(_scan_kernel, nb=nb),
        grid=(batch, nblk),
        in_specs=m_in_specs + s_in_specs,
        out_specs=m_out_specs + s_out_specs,
        out_shape=m_out_shape + s_out_shape,
        scratch_shapes=m_scratch + s_scratch,
        compiler_params=pltpu.CompilerParams(
            dimension_semantics=("arbitrary", "arbitrary"), vmem_limit_bytes=VMEM_LIMIT),
        name="scans",
    )(qkv, qkv, gates, gates, m_bias, m_sel, c0, m0, xs, bc, xs, bc, gates, gates, dtb_row, s_sel, alog8, dskip_exp, h0)


def _mix_kernel(x_ref, hf_ref, hb_ref, misc_ref, pprev_ref, pnext_ref, yf_ref, yb_ref, mod_ref,
                wout_ref, wpool_ref, pscale_ref, mnorm_ref, snorm_ref, wsp_ref, bsp_ref, ln_g_ref, ln_b_ref,
                mean_sel_ref, o_ref, cat_ref, pad_ref, *, seq, per_seq):
    rows = x_ref.shape[0]
    i = pl.program_id(0)
    hs = hf_ref[...] + hb_ref[...]
    mu = _dot_sel2(hs, mean_sel_ref[...])
    hc = hs - mu
    var = _dot_sel2(hc * hc, mean_sel_ref[...])
    y_m = _sigmoid(misc_ref[:, 0:256]) * (hc * lax.rsqrt(var + LN_EPS) * mnorm_ref[...])
    cat_ref[:, 0:256] = y_m.astype(BF16)
    blk = i % per_seq
    xp = misc_ref[:, 256:512]
    pad_ref[0:SUBLANES, :] = jnp.where(blk == 0, 0.0, pprev_ref[...])
    pad_ref[SUBLANES:SUBLANES + rows, :] = xp
    pad_ref[SUBLANES + rows:2 * SUBLANES + rows, :] = jnp.where(blk == per_seq - 1, 0.0, pnext_ref[...])
    n_pad = rows + 2 * SUBLANES
    back = lambda a, k: pltpu.roll(a, k, 0)
    ahead = lambda a, k: pltpu.roll(a, n_pad - k, 0)
    s2 = pad_ref[...]
    s2 = s2 + back(s2, 1)
    s4 = s2 + back(s2, 2)
    lo_half = _iota((n_pad, LANES), 1) < HD
    s4_hi = s4[:, LANES:2 * LANES]
    s8 = s4_hi + back(s4_hi, 4)
    s16 = s8 + back(s8, 8)
    wsum = jnp.concatenate([jnp.where(lo_half, s2[:, 0:LANES], ahead(s4[:, 0:LANES], 1)),
                            jnp.where(lo_half, ahead(s8, 3), ahead(s16, 7))], axis=1)
    wsum = wsum[SUBLANES:SUBLANES + rows, :]
    win = jnp.left_shift(2, _iota((SUBLANES, 256), 1) // HD)
    half = win // 2
    pooled = wsum * (1.0 / win[0:1, :].astype(F32)) - xp

    def edge(r0, t0):
        t = t0 + _iota((SUBLANES, 256), 0)
        cnt = jnp.clip(t - half + win, 0, seq) - jnp.clip(t - half, 0, seq)
        return wsum[r0:r0 + SUBLANES, :] / cnt.astype(F32) - xp[r0:r0 + SUBLANES, :]

    top = jnp.where(blk == 0, edge(0, 0), pooled[0:SUBLANES, :])
    bot = jnp.where(blk == per_seq - 1, edge(rows - SUBLANES, seq - SUBLANES), pooled[rows - SUBLANES:rows, :])
    pooled = jnp.concatenate([top, pooled[SUBLANES:rows - SUBLANES, :], bot], axis=0)
    y_p = _dot(pooled.astype(BF16), wpool_ref[...]) * pscale_ref[...]
    cat_ref[:, 256:512] = y_p.astype(BF16)
    vn = _ln(misc_ref[:, 768:1024]).astype(BF16)
    bd = (_iota((512, 256), 0) // G_CHUNK) == (_iota((512, 256), 1) // HD)
    for c in range(rows // G_CHUNK):
        r = slice(c * G_CHUNK, (c + 1) * G_CHUNK)
        vbd = jnp.where(bd, _tile4(vn[r, :]), jnp.zeros((), BF16))
        mixed = _dot(wsp_ref[...], vbd) + bsp_ref[...]
        cat_ref[r, 512:768] = (misc_ref[r, 512:768] * mixed).astype(BF16)
    y = (yf_ref[...] + yb_ref[...]) * _silu(misc_ref[:, 1024:1280])
    for g in range(2):
        yg = y[:, g * LANES:(g + 1) * LANES]
        yg = yg * lax.rsqrt(jnp.mean(yg * yg, -1, keepdims=True) + LN_EPS)
        cat_ref[:, 768 + g * LANES:768 + (g + 1) * LANES] = (yg * snorm_ref[:, g * LANES:(g + 1) * LANES]).astype(BF16)
    mix = _dot(cat_ref[...], wout_ref[...])
    gate = mod_ref[0, 2:3, :]
    o_ref[...] = _ln(ALPHA * x_ref[...] + gate * mix) * ln_g_ref[...] + ln_b_ref[...]


def _mix_call(x, hf, hb, misc, yf, yb, mod_l, w_out, w_pool_bd, pool_scale, mnorm, snorm, wsp_cat, bsp_exp,
              ln_g, ln_b, mean_sel, seq, rows_blk, cond_of_tile):
    rows = x.shape[0]
    nt = rows // rows_blk
    per_seq = seq // rows_blk
    sub_per_blk = rows_blk // SUBLANES
    n_sub = rows // SUBLANES
    row_spec = lambda w: pl.BlockSpec((rows_blk, w), lambda i: (i, 0))
    const = lambda shape: pl.BlockSpec(shape, lambda i: tuple(0 for _ in shape))
    return pl.pallas_call(
        functools.partial(_mix_kernel, seq=seq, per_seq=per_seq),
        grid=(nt,),
        in_specs=[row_spec(D_MODEL), row_spec(256), row_spec(256), row_spec(MISC_W),
                  pl.BlockSpec((SUBLANES, 256), lambda i: (jnp.maximum(i * sub_per_blk - 1, 0), 1)),
                  pl.BlockSpec((SUBLANES, 256), lambda i: (jnp.minimum((i + 1) * sub_per_blk, n_sub - 1), 1)),
                  row_spec(256), row_spec(256),
                  pl.BlockSpec((1, 6, D_MODEL), lambda i: (cond_of_tile(i), 0, 0)),
                  const((D_MODEL, D_MODEL)), const((256, 256)), const((1, 256)), const((1, 256)), const((1, 256)),
                  const((G_CHUNK, 512)), const((G_CHUNK, 256)), const((1, D_MODEL)), const((1, D_MODEL)),
                  const((256, 256))],
        out_specs=row_spec(D_MODEL),
        out_shape=jax.ShapeDtypeStruct((rows, D_MODEL), F32),
        scratch_shapes=[pltpu.VMEM((rows_blk, D_MODEL), BF16), pltpu.VMEM((rows_blk + 2 * SUBLANES, 256), F32)],
        compiler_params=pltpu.CompilerParams(dimension_semantics=("arbitrary",), vmem_limit_bytes=VMEM_LIMIT),
        name="mix_outproj",
    )(x, hf, hb, misc, misc, misc, yf, yb, mod_l, w_out, w_pool_bd, pool_scale, mnorm, snorm, wsp_cat, bsp_exp,
      ln_g, ln_b, mean_sel)


def _ffn_kernel(x_ref, xprev_ref, xnext_ref, mod_ref, wup_ref, cw_ref, cb_ref, wdown_ref, ln_g_ref, ln_b_ref,
                o_ref, h_ref, act_ref, *, per_seq):
    rows = x_ref.shape[0]
    blk = pl.program_id(0) % per_seq
    shift = mod_ref[0, 3:4, :]
    scale = mod_ref[0, 4:5, :]
    gate = mod_ref[0, 5:6, :]
    modulate = lambda v: (_ln(v) * (1.0 + scale) + shift).astype(BF16)
    x = x_ref[...]
    base = 2 * SUBLANES
    n_pad = rows + 2 * base
    hp = modulate(jnp.concatenate([xprev_ref[...], xprev_ref[...]], axis=0))
    hn = modulate(jnp.concatenate([xnext_ref[...], xnext_ref[...]], axis=0))
    h_ref[0:base, :] = jnp.where(blk > 0, hp, jnp.zeros((), BF16))
    h_ref[base:base + rows, :] = modulate(x)
    h_ref[base + rows:n_pad, :] = jnp.where(blk < per_seq - 1, hn, jnp.zeros((), BF16))
    h = h_ref[...]

    def conv(u, col):
        up = pltpu.roll(u, 1, 0)[base:base + rows, :]
        un = pltpu.roll(u, n_pad - 1, 0)[base:base + rows, :]
        return (cw_ref[0:1, col] * up + cw_ref[1:2, col] * u[base:base + rows, :] + cw_ref[2:3, col] * un
                + cb_ref[:, col])

    for c in range(D_FF // FF_CHUNK):
        cg = slice(c * FF_CHUNK, (c + 1) * FF_CHUNK)
        cv = slice(D_FF + c * FF_CHUNK, D_FF + (c + 1) * FF_CHUNK)
        g = conv(_dot(h, wup_ref[:, cg]), cg)
        val = conv(_dot(h, wup_ref[:, cv]), cv)
        act_ref[:, cg] = (_silu(g) * val).astype(BF16)
    ffn = _dot(act_ref[...], wdown_ref[...])
    o_ref[...] = _ln(ALPHA * x + gate * ffn) * ln_g_ref[...] + ln_b_ref[...]


def _ffn_call(x, mod_l, w_up, conv_w, conv_b, w_down, ln_g, ln_b, seq, rows_blk, cond_of_tile):
    rows = x.shape[0]
    nt = rows // rows_blk
    per_seq = seq // rows_blk
    sub_per_blk = rows_blk // SUBLANES
    n_sub = rows // SUBLANES
    row_spec = lambda w: pl.BlockSpec((rows_blk, w), lambda i: (i, 0))
    const = lambda shape: pl.BlockSpec(shape, lambda i: tuple(0 for _ in shape))
    return pl.pallas_call(
        functools.partial(_ffn_kernel, per_seq=per_seq),
        grid=(nt,),
        in_specs=[row_spec(D_MODEL),
                  pl.BlockSpec((SUBLANES, D_MODEL), lambda i: (jnp.maximum(i * sub_per_blk - 1, 0), 0)),
                  pl.BlockSpec((SUBLANES, D_MODEL), lambda i: (jnp.minimum((i + 1) * sub_per_blk, n_sub - 1), 0)),
                  pl.BlockSpec((1, 6, D_MODEL), lambda i: (cond_of_tile(i), 0, 0)),
                  const((D_MODEL, 2 * D_FF)), const((3, 2 * D_FF)), const((1, 2 * D_FF)),
                  const((D_FF, D_MODEL)), const((1, D_MODEL)), const((1, D_MODEL))],
        out_specs=row_spec(D_MODEL),
        out_shape=jax.ShapeDtypeStruct((rows, D_MODEL), F32),
        scratch_shapes=[pltpu.VMEM((rows_blk + 4 * SUBLANES, D_MODEL), BF16), pltpu.VMEM((rows_blk, D_FF), BF16)],
        compiler_params=pltpu.CompilerParams(dimension_semantics=("arbitrary",), vmem_limit_bytes=VMEM_LIMIT),
        name="conv_ffn",
    )(x, x, x, mod_l, w_up, conv_w, conv_b, w_down, ln_g, ln_b)


def _grid_pos_embed(rows, dim):
    quarter = dim // 4
    freq = 1.0 / (POS_BASE ** (jnp.arange(quarter, dtype=F32) / quarter))
    r = jnp.repeat(jnp.arange(rows, dtype=F32), GRID_W)
    col = jnp.tile(jnp.arange(GRID_W, dtype=F32), rows)

    def enc(pos):
        ang = pos[:, None] * freq[None, :]
        return jnp.concatenate([jnp.sin(ang), jnp.cos(ang)], -1)

    return jnp.concatenate([enc(r), enc(col)], -1)


def _selector(n_quant, pad_rows, widths):
    sel = np.zeros((N_DIR, 3 * pad_rows, sum(widths)), np.float32)
    offs = np.concatenate([[0], np.cumsum(widths)])
    for d in range(N_DIR):
        for p in range(3):
            for q in range(n_quant):
                per_head = widths[q] // HEADS
                for h in range(HEADS):
                    lo = offs[q] + h * per_head
                    sel[d, p * pad_rows + q * SUBLANES + d * HEADS + h, lo:lo + per_head] = 1.0
    return jnp.asarray(sel, BF16)


def _constants():
    m_sel = _selector(4, 32, [HS, 256, 256, 256])
    s_sel = _selector(3, 32, [HS, 256, 256])
    mean_sel = np.kron(np.eye(HEADS, dtype=np.float32), np.full((HD, HD), 1.0 / HD, np.float32))
    return m_sel, s_sel, jnp.asarray(mean_sel, BF16)


def _permute_w_in(w):
    pad = jnp.zeros((D_MODEL, LANES - 24), w.dtype)
    return jnp.concatenate([w[:, 0:768], w[:, 768:1024], w[:, 1040:1296], w[:, 1296:1808], w[:, 1808:2064],
                            w[:, 2064:2832], w[:, 1024:1040], w[:, 2832:2840], pad], axis=1).astype(BF16)


def _block_diag4(blocks):
    a, b = blocks.shape[1:]
    eye = jnp.eye(HEADS, dtype=blocks.dtype)
    return (eye[:, None, :, None] * blocks[:, :, None, :]).reshape(HEADS * a, HEADS * b)


def _dir_head_rows(v):
    rep = jnp.broadcast_to(v[..., None], v.shape + (LANES,))
    zeros = jnp.zeros_like(rep[..., 0, :, :])
    return jnp.stack([jnp.concatenate([rep[..., 0, :, :], zeros], axis=-2),
                      jnp.concatenate([zeros, rep[..., 1, :, :]], axis=-2)], axis=-3)


def _pack_mlstm_state(c, n, m):
    eye = jnp.eye(HEADS, dtype=F32)
    cbd = (eye[None, None, :, None, :, None] * c[:, :, :, :, None, :]).reshape(c.shape[0], N_DIR, 256, 256)
    nbd = jnp.broadcast_to((eye[None, None, :, None, :, None] * n[:, :, :, :, None, None]),
                           (c.shape[0], N_DIR, HEADS, HD, HEADS, HD)).reshape(c.shape[0], N_DIR, 256, 256)
    return jnp.concatenate([cbd, nbd], axis=-1), _dir_head_rows(m)


def _unpack_mlstm_state(cout, mout):
    b = cout.shape[0]
    c = cout[..., 0:HD].reshape(b, N_DIR, HEADS, HD, HD)
    n = cout[..., HD].reshape(b, N_DIR, HEADS, HD)
    m = jnp.stack([mout[:, d, d * HEADS:(d + 1) * HEADS, 0] for d in range(N_DIR)], axis=1)
    return c, n, m


def _pack_ssd_state(s):
    b = s.shape[0]
    st = jnp.transpose(s, (0, 1, 4, 2, 3))
    grp = (jnp.arange(HEADS) // 2)[None, :] == jnp.arange(2)[:, None]
    full = st[:, :, None, :, :, :] * grp[None, None, :, None, :, None].astype(F32)
    return full.reshape(b, N_DIR, 256, 256)


def _unpack_ssd_state(hp):
    heads = [jnp.swapaxes(hp[:, :, (h // 2) * S_STATE:(h // 2 + 1) * S_STATE, h * HD:(h + 1) * HD], -1, -2)
             for h in range(HEADS)]
    return jnp.stack(heads, axis=2)


def _layer(x, pos, mod_l, p, m_state, s_state, batch, seq, rows_blk, cond_of_tile, consts):
    m_sel, s_sel, mean_sel = consts
    outs = _inproj_call(x, pos, mod_l, p["w_in"], p["ssd_conv_w"], p["ssd_conv_b"], seq, rows_blk, cond_of_tile)
    if pos is not None:
        qkv, misc, xs, bc, gates, x = outs
    else:
        qkv, misc, xs, bc, gates = outs
    hf, hb, cout, mout, yf, yb, hs = _scan_call(qkv, xs, bc, gates, p["m_bias"], m_sel, m_state[0], m_state[1],
                                                p["dt_bias"], s_sel, p["a_log"], p["d_skip"], s_state, batch, seq,
                                                rows_blk)
    x1 = _mix_call(x, hf, hb, misc, yf, yb, mod_l, p["w_out"], p["w_pool"], p["pool_scale"], p["mnorm"],
                   p["snorm"], p["wsp"], p["bsp"], p["ln1_g"], p["ln1_b"], mean_sel, seq, rows_blk, cond_of_tile)
    x2 = _ffn_call(x1, mod_l, p["ffn_w_up"], p["ffn_conv_w"], p["ffn_conv_b"], p["ffn_w_down"], p["ln2_g"],
                   p["ln2_b"], seq, rows_blk, cond_of_tile)
    return x2, (cout, mout), hs


def kernel(x_prompt, x_sample, state_mlstm_c, state_mlstm_n, state_mlstm_m, state_ssd, c, c_ctx, w_ada, b_ada, w_in, b_igate, b_fgate, mlstm_norm_g, w_pool, pool_scale, w_spatial, b_spatial, ssd_conv_w, ssd_conv_b, ssd_dt_bias, ssd_a_log, ssd_d, ssd_norm_g, w_out, ln1_g, ln1_b, ffn_w_up, ffn_conv_w, ffn_conv_b, ffn_w_down, ln2_g, ln2_b):
    n_ctx, t_ctx, _ = x_prompt.shape
    n_lat, t_lat, _ = x_sample.shape
    consts = _constants()
    pos = _grid_pos_embed(t_lat // GRID_W, D_MODEL)

    cond = jnp.zeros((N_COND, D_MODEL), F32).at[0:n_lat].set(c).at[CTX_COND].set(c_ctx)
    mod = _ada_call(cond, w_ada, b_ada).reshape(DEPTH, N_COND, 6, D_MODEL)

    ctx_blk = min(t_ctx, 256)
    lat_blk = min(t_lat, 512)
    lat_per_seq = t_lat // lat_blk
    ctx_cond = lambda i: CTX_COND
    lat_cond = lambda i: i // lat_per_seq

    y_p = x_prompt.reshape(n_ctx * t_ctx, D_MODEL)
    y_s = x_sample.reshape(n_lat * t_lat, D_MODEL)
    zero_m = (jnp.zeros((n_ctx, N_DIR, 256, 512), F32), jnp.zeros((n_ctx, N_DIR, SUBLANES, LANES), F32))
    zero_s = jnp.zeros((n_ctx, N_DIR, 256, 256), F32)
    new_c, new_n, new_m, new_s = [], [], [], []
    for l in range(DEPTH):
        m_bias = jnp.zeros((1, LANES), F32).at[0, 0:8].set(b_igate[l].reshape(-1)).at[0, 8:16].set(b_fgate[l].reshape(-1))
        dt_bias = jnp.zeros((1, LANES), F32).at[0, 16:24].set(ssd_dt_bias[l].reshape(-1))
        p = {
            "w_in": _permute_w_in(w_in[l]),
            "m_bias": m_bias,
            "dt_bias": dt_bias,
            "a_log": jnp.broadcast_to(ssd_a_log[l].reshape(N_DIR * HEADS, 1), (N_DIR * HEADS, LANES)),
            "d_skip": jnp.repeat(ssd_d[l], HD)[None, :],
            "ssd_conv_w": ssd_conv_w[l],
            "ssd_conv_b": ssd_conv_b[l][None, :],
            "w_out": w_out[l].astype(BF16),
            "w_pool": _block_diag4(w_pool[l]).astype(BF16),
            "pool_scale": pool_scale[l][None, :],
            "mnorm": mlstm_norm_g[l][None, :],
            "snorm": ssd_norm_g[l][None, :],
            "wsp": jnp.transpose(w_spatial[l], (1, 0, 2)).reshape(G_CHUNK, HEADS * G_CHUNK).astype(BF16),
            "bsp": jnp.repeat(b_spatial[l].T, HD, axis=-1),
            "ln1_g": ln1_g[l][None, :], "ln1_b": ln1_b[l][None, :],
            "ffn_w_up": ffn_w_up[l].astype(BF16),
            "ffn_conv_w": ffn_conv_w[l],
            "ffn_conv_b": ffn_conv_b[l][None, :],
            "ffn_w_down": ffn_w_down[l].astype(BF16),
            "ln2_g": ln2_g[l][None, :], "ln2_b": ln2_b[l][None, :],
        }
        y_p, (cout, mout), hs = _layer(y_p, None, mod[l], p, zero_m, zero_s, n_ctx, t_ctx, ctx_blk, ctx_cond, consts)
        cc, nn, mm = _unpack_mlstm_state(cout, mout)
        new_c.append(cc)
        new_n.append(nn)
        new_m.append(mm)
        new_s.append(_unpack_ssd_state(hs))
        lat_m = _pack_mlstm_state(state_mlstm_c[:, l], state_mlstm_n[:, l], state_mlstm_m[:, l])
        y_s, _, _ = _layer(y_s, pos if l == 0 else None, mod[l], p, lat_m, _pack_ssd_state(state_ssd[:, l]),
                           n_lat, t_lat, lat_blk, lat_cond, consts)
    return (y_p.reshape(n_ctx, t_ctx, D_MODEL), y_s.reshape(n_lat, t_lat, D_MODEL),
            jnp.stack(new_c, 1), jnp.stack(new_n, 1), jnp.stack(new_m, 1), jnp.stack(new_s, 1))
```

```python
import functools

import numpy as np
import jax
import jax.numpy as jnp
from jax import lax
from jax.experimental import pallas as pl
from jax.experimental.pallas import tpu as pltpu

F32 = jnp.float32
BF16 = jnp.bfloat16

D_MODEL = 1024
DEPTH = 2
GRID_W = 64
POS_BASE = 10000.0
W_GROUP = 256
N_DIR = 2
HEADS = 4
HD = 64
G_CHUNK = 128
S_STATE = 128
S_XBC = 768
D_FF = 2816
FF_CHUNK = 256
ALPHA = (2 * DEPTH) ** 0.25
LN_EPS = 1e-5
N_COND = 16
CTX_COND = 8
LANES = 128
SUBLANES = 8
CHUNK = LANES
HS = HEADS * CHUNK
SCAN_RADIX = 8
QKV_W = 768
MISC_W = 1280
IN_W = QKV_W + MISC_W + S_XBC + LANES
M_EXP_W = HS + 3 * W_GROUP
S_EXP_W = HS + 2 * W_GROUP
M_SEL_K = 3 * 32
S_SEL_K = 3 * 32
NB_MAX = 4
VMEM_LIMIT = 56 * 1024 * 1024


def _dot(a, b):
    return jnp.dot(a, b, preferred_element_type=F32)


def _dot_nt(a, b):
    return lax.dot_general(a, b, (((1,), (1,)), ((), ())), preferred_element_type=F32)


def _dot_tn(a, b):
    return lax.dot_general(a, b, (((0,), (0,)), ((), ())), preferred_element_type=F32)


def _split3(x):
    hi = x.astype(BF16)
    r1 = x - hi.astype(F32)
    mid = r1.astype(BF16)
    lo = (r1 - mid.astype(F32)).astype(BF16)
    return hi, mid, lo


def _dot_sel2(x, sel):
    hi = x.astype(BF16)
    lo = (x - hi.astype(F32)).astype(BF16)
    return _dot(hi, sel) + _dot(lo, sel)


def _expand(rows_f32, sel):
    hi, mid, lo = _split3(rows_f32)
    return _dot_tn(jnp.concatenate([hi, mid, lo], axis=0), sel)


def _ln(x):
    mu = jnp.mean(x, -1, keepdims=True)
    xc = x - mu
    var = jnp.mean(xc * xc, -1, keepdims=True)
    return xc * lax.rsqrt(var + LN_EPS)


def _sigmoid(x):
    return jax.nn.sigmoid(x)


def _silu(x):
    return x * _sigmoid(x)


def _softplus(x):
    return jnp.maximum(x, 0.0) + jnp.log1p(jnp.exp(-jnp.abs(x)))


def _log_sigmoid(x):
    return jnp.minimum(x, 0.0) - jnp.log1p(jnp.exp(-jnp.abs(x)))


def _iota(shape, dim):
    return lax.broadcasted_iota(jnp.int32, shape, dim)


def _lane_scan_pair(xf, xr, op, ident):
    pos_f = _iota(xf.shape, 1)
    pos_r = _iota(xr.shape, 1)
    k = 1
    while k < LANES:
        acc_f, acc_r = xf, xr
        for mult in range(1, SCAN_RADIX):
            s = k * mult
            if s >= LANES:
                break
            acc_f = op(acc_f, jnp.where(pos_f >= s, pltpu.roll(xf, s, 1), ident))
            acc_r = op(acc_r, jnp.where(pos_r < LANES - s, pltpu.roll(xr, LANES - s, 1), ident))
        xf, xr = acc_f, acc_r
        k *= SCAN_RADIX
    return xf, xr


def _lane_bcast(x, lane):
    return jnp.broadcast_to(x[:, lane:lane + 1], x.shape)


def _stack_chunks(gt, r0, nb):
    return jnp.concatenate([gt[r0:r0 + SUBLANES, c * CHUNK:(c + 1) * CHUNK] for c in range(nb)], axis=0)


def _unstack_chunks(x, nb):
    return jnp.concatenate([x[c * SUBLANES:(c + 1) * SUBLANES] for c in range(nb)], axis=1)


def _head_rows(x, d):
    return jnp.concatenate([x[d * HEADS + h:d * HEADS + h + 1, :] for h in range(HEADS)], axis=1)


def _head_scalar_row(x, d):
    lo = _iota((1, LANES), 1) < HD
    r = d * HEADS
    return jnp.concatenate([jnp.where(lo, x[r:r + 1, :], x[r + 1:r + 2, :]),
                            jnp.where(lo, x[r + 2:r + 3, :], x[r + 3:r + 4, :])], axis=1)


def _tile4(x):
    return jnp.concatenate([x, x, x, x], axis=0)


def _ada_kernel(cond_ref, w_ref, b_ref, o_ref):
    s = _silu(cond_ref[...]).astype(BF16)
    o_ref[0] = _dot(s, w_ref[0].astype(BF16)) + b_ref[0]


def _ada_call(cond, w_ada, b_ada):
    n_out = w_ada.shape[-1]
    tn = 1536
    return pl.pallas_call(
        _ada_kernel,
        grid=(DEPTH, n_out // tn),
        in_specs=[
            pl.BlockSpec((N_COND, D_MODEL), lambda l, n: (0, 0)),
            pl.BlockSpec((1, D_MODEL, tn), lambda l, n: (l, 0, n)),
            pl.BlockSpec((1, 1, tn), lambda l, n: (l, 0, n)),
        ],
        out_specs=pl.BlockSpec((1, N_COND, tn), lambda l, n: (l, 0, n)),
        out_shape=jax.ShapeDtypeStruct((DEPTH, N_COND, n_out), F32),
        compiler_params=pltpu.CompilerParams(
            dimension_semantics=("arbitrary", "arbitrary"), vmem_limit_bytes=VMEM_LIMIT),
        name="ada_mod",
    )(cond, w_ada, b_ada.reshape(DEPTH, 1, n_out))


def _inproj_kernel(*refs, has_pos, per_seq):
    if has_pos:
        (x_ref, xp_ref, xn_ref, pos_ref, pp_ref, pn_ref, mod_ref, w_ref, cw_ref, cb_ref,
         qkv_ref, misc_ref, xs_ref, bc_ref, gates_ref, xres_ref, hx_ref) = refs
        x = x_ref[...] + pos_ref[...]
        x_prev = xp_ref[...] + pp_ref[...]
        x_next = xn_ref[...] + pn_ref[...]
        xres_ref[...] = x
    else:
        (x_ref, xp_ref, xn_ref, mod_ref, w_ref, cw_ref, cb_ref,
         qkv_ref, misc_ref, xs_ref, bc_ref, gates_ref, hx_ref) = refs
        x = x_ref[...]
        x_prev = xp_ref[...]
        x_next = xn_ref[...]
    rows = x.shape[0]
    blk = pl.program_id(0) % per_seq
    shift = mod_ref[0, 0:1, :]
    scale = mod_ref[0, 1:2, :]
    modulate = lambda v: (_ln(v) * (1.0 + scale) + shift).astype(BF16)
    base = 2 * SUBLANES
    n_pad = rows + 2 * base
    hx_ref[0:base, :] = jnp.where(blk > 0, modulate(jnp.concatenate([x_prev, x_prev], axis=0)), jnp.zeros((), BF16))
    hx_ref[base:base + rows, :] = modulate(x)
    hx_ref[base + rows:n_pad, :] = jnp.where(blk < per_seq - 1, modulate(jnp.concatenate([x_next, x_next], axis=0)),
                                             jnp.zeros((), BF16))
    u = _dot(hx_ref[...], w_ref[:, QKV_W + MISC_W:QKV_W + MISC_W + S_XBC])
    act = _silu(cw_ref[0:1, :] * pltpu.roll(u, 1, 0)[base:base + rows, :] + cw_ref[1:2, :] * u[base:base + rows, :]
                + cw_ref[2:3, :] * pltpu.roll(u, n_pad - 1, 0)[base:base + rows, :] + cb_ref[...])
    xs_ref[...] = act[:, 0:256]
    bc_ref[...] = act[:, 256:768].astype(BF16)
    h = hx_ref[base:base + rows, :]
    qkv = _dot(h, w_ref[:, 0:QKV_W])
    qkv_ref[:, 0:256] = qkv[:, 0:256].astype(BF16)
    qkv_ref[:, 256:512] = (qkv[:, 256:512] * (HD ** -0.5)).astype(BF16)
    qkv_ref[:, 512:768] = qkv[:, 512:768].astype(BF16)
    misc_ref[...] = _dot(h, w_ref[:, QKV_W:QKV_W + MISC_W])
    gates_ref[...] = _dot(h, w_ref[:, QKV_W + MISC_W + S_XBC:IN_W])


def _inproj_call(x, pos, mod_l, w_in_p, conv_w, conv_b, seq, rows_blk, cond_of_tile):
    rows = x.shape[0]
    nt = rows // rows_blk
    per_seq = seq // rows_blk
    sub_per_blk = rows_blk // SUBLANES
    has_pos = pos is not None
    row_spec = lambda w: pl.BlockSpec((rows_blk, w), lambda i: (i, 0))
    const = lambda shape: pl.BlockSpec(shape, lambda i: tuple(0 for _ in shape))
    halo = lambda n_sub, blk_of, step: pl.BlockSpec(
        (SUBLANES, D_MODEL), lambda i: (jnp.clip(blk_of(i) * sub_per_blk + step, 0, n_sub - 1), 0))
    in_specs = [row_spec(D_MODEL), halo(rows // SUBLANES, lambda i: i, -1),
                halo(rows // SUBLANES, lambda i: i, sub_per_blk)]
    args = [x, x, x]
    if has_pos:
        in_specs += [pl.BlockSpec((rows_blk, D_MODEL), lambda i: (i % per_seq, 0)),
                     halo(seq // SUBLANES, lambda i: i % per_seq, -1),
                     halo(seq // SUBLANES, lambda i: i % per_seq, sub_per_blk)]
        args += [pos, pos, pos]
    in_specs += [pl.BlockSpec((1, 6, D_MODEL), lambda i: (cond_of_tile(i), 0, 0)), const((D_MODEL, IN_W)),
                 const((3, S_XBC)), const((1, S_XBC))]
    args += [mod_l, w_in_p, conv_w, conv_b]
    out_specs = [row_spec(QKV_W), row_spec(MISC_W), row_spec(256), row_spec(512), row_spec(LANES)]
    out_shape = [jax.ShapeDtypeStruct((rows, QKV_W), BF16), jax.ShapeDtypeStruct((rows, MISC_W), F32),
                 jax.ShapeDtypeStruct((rows, 256), F32), jax.ShapeDtypeStruct((rows, 512), BF16),
                 jax.ShapeDtypeStruct((rows, LANES), F32)]
    if has_pos:
        out_specs.append(row_spec(D_MODEL))
        out_shape.append(jax.ShapeDtypeStruct((rows, D_MODEL), F32))
    return pl.pallas_call(
        functools.partial(_inproj_kernel, has_pos=has_pos, per_seq=per_seq),
        grid=(nt,), in_specs=in_specs, out_specs=out_specs, out_shape=out_shape,
        scratch_shapes=[pltpu.VMEM((rows_blk + 4 * SUBLANES, D_MODEL), BF16)],
        compiler_params=pltpu.CompilerParams(dimension_semantics=("arbitrary",), vmem_limit_bytes=VMEM_LIMIT),
        name="inproj",
    )(*args)


def _mlstm_stages(qkvf_ref, qkvb_ref, gf_ref, gb_ref, bias_ref, sel_ref, c0_ref, m0_ref,
                  hf_ref, hb_ref, cout_ref, mout_ref, caug_ref, m_ref, exp_ref, grow_ref, nb):
    lane = _iota((CHUNK, HS), 1)
    row = _iota((CHUNK, HS), 0)
    causal = ((lane % CHUNK) <= row, (lane % CHUNK) >= row)
    bd_hs = (_iota((HS, 256), 0) // CHUNK) == (_iota((HS, 256), 1) // HD)
    bd = (_iota((256, 256), 0) // HD) == (_iota((256, 256), 1) // HD)
    bd2 = jnp.concatenate([bd, bd], axis=1)
    ones_rows = jnp.ones((CHUNK, 256), BF16)
    zero = jnp.zeros((), BF16)

    qkv_refs = (qkvf_ref, qkvb_ref)
    h_refs = (hf_ref, hb_ref)

    def init():
        caug_ref[...] = jnp.zeros(caug_ref.shape, F32) if c0_ref is None else c0_ref[0]
        m_ref[...] = jnp.zeros(m_ref.shape, F32) if m0_ref is None else m0_ref[0]

    def chunk(i, d):
        c = (nb - 1 - i) if d else i
        rows = slice(c * CHUNK, (c + 1) * CHUNK)
        q = qkv_refs[d][rows, 0:256]
        k = qkv_refs[d][rows, 256:512]
        v = qkv_refs[d][rows, 512:768]
        s = _dot_nt(q, jnp.where(bd_hs, _tile4(k), zero))
        expo = exp_ref[d, rows, 0:HS] + grow_ref[d, c:c + 1, :]
        p = s * jnp.exp(jnp.where(causal[d], expo, -jnp.inf))
        den_intra = jnp.concatenate(
            [jnp.broadcast_to(jnp.sum(p[:, h * CHUNK:(h + 1) * CHUNK], axis=-1, keepdims=True), (CHUNK, HD))
             for h in range(HEADS)], axis=1)
        num_intra = _dot(p.astype(BF16), jnp.where(bd_hs, _tile4(v), zero))
        qc = _dot(q, caug_ref[d].astype(BF16))
        ew = exp_ref[d, rows, HS:HS + 256]
        num = num_intra + ew * qc[:, 0:256]
        den = den_intra + ew * qc[:, 256:512]
        h_refs[d][rows, :] = num / jnp.maximum(jnp.abs(den), jnp.exp(exp_ref[d, rows, HS + 256:HS + 512]))
        kw = (k.astype(F32) * exp_ref[d, rows, HS + 512:HS + 768]).astype(BF16)
        u = _dot_tn(kw, jnp.concatenate([v, ones_rows], axis=1))
        wc = grow_ref[d, NB_MAX + c:NB_MAX + c + 1, 0:256]
        caug_ref[d] = jnp.concatenate([wc, wc], axis=1) * caug_ref[d] + jnp.where(bd2, u, 0.0)

    def final():
        mout_ref[0] = m_ref[...]
        for d in range(N_DIR):
            for h in range(HEADS):
                r = slice(h * HD, (h + 1) * HD)
                cout_ref[0, d, r, 0:HD] = caug_ref[d, r, h * HD:(h + 1) * HD]
                cout_ref[0, d, r, HD:2 * HD] = caug_ref[d, r, 256 + h * HD:256 + (h + 1) * HD]

    return init, chunk, final


def _scan_prepare(g_refs, mbias_ref, dtb_ref, alog_ref, msel_ref, ssel_ref, m_ref, nb,
                  m_exp_ref, m_grow_ref, s_exp_ref, s_grow_ref, s_dtrow_ref):
    bias = mbias_ref[...] + dtb_ref[...]
    gts = [(g_refs[d][...] + bias).T for d in range(N_DIR)]
    a = jnp.concatenate([-jnp.exp(alog_ref[...])] * nb, axis=0)
    n8 = nb * SUBLANES
    lf = [_log_sigmoid(_stack_chunks(gts[d], 8, nb)) for d in range(N_DIR)]
    dt = [_softplus(_stack_chunks(gts[d], 16, nb)) for d in range(N_DIR)]
    csum = _lane_scan_pair(jnp.concatenate([lf[0], dt[0] * a], axis=0), jnp.concatenate([lf[1], dt[1] * a], axis=0),
                           jnp.add, 0.0)
    bcs = [csum[d][0:n8] for d in range(N_DIR)]
    lc = [csum[d][n8:2 * n8] for d in range(N_DIR)]
    g = [_stack_chunks(gts[d], 0, nb) - bcs[d] for d in range(N_DIR)]
    gmax = _lane_scan_pair(g[0], g[1], jnp.maximum, -jnp.inf)
    for d in range(N_DIR):
        last = 0 if d else LANES - 1
        g_last = _lane_bcast(gmax[d], last)
        b_last = _lane_bcast(bcs[d], last)
        lc_last = _lane_bcast(lc[d], last)
        m_prev = m_ref[d]
        m_prevs = [None] * nb
        m_lasts = [None] * nb
        for c in (range(nb - 1, -1, -1) if d else range(nb)):
            r = slice(c * SUBLANES, (c + 1) * SUBLANES)
            m_prevs[c] = m_prev
            m_lasts[c] = jnp.maximum(m_prev, g_last[r])
            m_grow_ref[d, c:c + 1, :] = _head_rows(g[d][r], d)
            s_grow_ref[d, c:c + 1, :] = _head_rows(-lc[d][r], d)
            s_dtrow_ref[d, c:c + 1, :] = _head_rows(dt[d][r], d)
            m_grow_ref[d, NB_MAX + c:NB_MAX + c + 1, 0:256] = _head_scalar_row(jnp.exp(m_prev - m_lasts[c]), d)
            s_grow_ref[d, NB_MAX + c:NB_MAX + c + 1, 0:256] = _head_scalar_row(jnp.exp(lc_last[r]), d)
            m_prev = b_last[r] + m_lasts[c]
        m_ref[d] = m_prev
        mp = jnp.concatenate(m_prevs, axis=0)
        ml = jnp.concatenate(m_lasts, axis=0)
        m_q = jnp.maximum(mp, gmax[d])
        m_quant = (-m_q, jnp.exp(mp - m_q), -(bcs[d] + m_q), jnp.exp(g[d] - ml))
        m_exp_ref[d] = _expand(jnp.concatenate([_unstack_chunks(q, nb) for q in m_quant], axis=0), msel_ref[d])
        s_quant = (lc[d], jnp.exp(lc[d]), jnp.exp(lc_last - lc[d]) * dt[d], jnp.zeros_like(dt[d]))
        s_exp_ref[d] = _expand(jnp.concatenate([_unstack_chunks(q, nb) for q in s_quant], axis=0), ssel_ref[d])


def _ssd_stages(xsf_ref, bcf_ref, xsb_ref, bcb_ref, gf_ref, gb_ref, dtb_ref, sel_ref, alog_ref, dskip_ref, h0_ref,
                yf_ref, yb_ref, hout_ref, h_ref, exp_ref, grow_ref, dtrow_ref, nb):
    lane = _iota((CHUNK, HS), 1)
    row = _iota((CHUNK, HS), 0)
    causal = ((lane % CHUNK) <= row, (lane % CHUNK) >= row)
    r_hs = _iota((HS, 256), 0)
    c_hs = _iota((HS, 256), 1)
    bd_hs = (r_hs // CHUNK) == (c_hs // HD)
    gmask_b = (r_hs // CHUNK // 2) == (c_hs // S_STATE)
    gmask_h = (_iota((256, 256), 0) // S_STATE) == (_iota((256, 256), 1) // HD // 2)
    zero = jnp.zeros((), BF16)

    xs_refs = (xsf_ref, xsb_ref)
    bc_refs = (bcf_ref, bcb_ref)
    y_refs = (yf_ref, yb_ref)

    def init():
        h_ref[...] = jnp.zeros(h_ref.shape, F32) if h0_ref is None else h0_ref[0]

    def chunk(i, d):
        c = (nb - 1 - i) if d else i
        rows = slice(c * CHUNK, (c + 1) * CHUNK)
        xs = xs_refs[d][rows, :]
        bm = bc_refs[d][rows, 0:256]
        cm = bc_refs[d][rows, 256:512]
        cb = _dot_nt(cm, jnp.where(gmask_b, _tile4(bm), zero))
        expo = exp_ref[d, rows, 0:HS] + grow_ref[d, c:c + 1, :]
        dmat = jnp.exp(jnp.where(causal[d], expo, -jnp.inf)) * dtrow_ref[d, c:c + 1, :]
        p = (cb * dmat).astype(BF16)
        xbd = jnp.where(bd_hs, _tile4(xs.astype(BF16)), zero)
        y = _dot(p, xbd) + exp_ref[d, rows, HS:HS + 256] * _dot(cm, h_ref[d].astype(BF16))
        if d == 0:
            y = y + dskip_ref[...] * xs
        y_refs[d][rows, :] = y
        xw = (xs * exp_ref[d, rows, HS + 256:HS + 512]).astype(BF16)
        u = _dot_tn(bm, xw)
        da = grow_ref[d, NB_MAX + c:NB_MAX + c + 1, 0:256]
        h_ref[d] = da * h_ref[d] + jnp.where(gmask_h, u, 0.0)

    def final():
        hout_ref[0] = h_ref[...]

    return init, chunk, final


N_M_IN, N_M_OUT, N_M_SCR = 8, 4, 4
N_S_IN, N_S_OUT, N_S_SCR = 11, 3, 4


def _scan_kernel(*refs, nb, zero_state):
    n_m_in = N_M_IN - 2 * zero_state
    n_s_in = N_S_IN - zero_state
    o = 0
    m_in = refs[o:o + n_m_in] + (None,) * (N_M_IN - n_m_in)
    o += n_m_in
    s_in = refs[o:o + n_s_in] + (None,) * (N_S_IN - n_s_in)
    o += n_s_in
    m_out = refs[o:o + N_M_OUT]
    o += N_M_OUT
    s_out = refs[o:o + N_S_OUT]
    o += N_S_OUT
    m_scr = refs[o:o + N_M_SCR]
    o += N_M_SCR
    s_scr = refs[o:o + N_S_SCR]
    stages = (_mlstm_stages(*m_in, *m_out, *m_scr, nb), _ssd_stages(*s_in, *s_out, *s_scr, nb))
    j = pl.program_id(1)
    nblk = pl.num_programs(1)

    @pl.when(j == 0)
    def _():
        for st in stages:
            st[0]()

    _scan_prepare((m_in[2], m_in[3]), m_in[4], s_in[6], s_in[8], m_in[5], s_in[7], m_scr[1], nb,
                  m_scr[2], m_scr[3], s_scr[1], s_scr[2], s_scr[3])
    for i in range(nb):
        for d in range(N_DIR):
            for st in stages:
                st[1](i, d)

    @pl.when(j == nblk - 1)
    def _():
        for st in stages:
            st[2]()


def _scan_call(qkv, xs, bc, gates, m_bias, m_sel, c0, m0, dtb_row, s_sel, alog8, dskip_exp, h0, batch, seq, rows_blk):
    rows = qkv.shape[0]
    nblk = seq // rows_blk
    nb = rows_blk // CHUNK
    fwd = lambda w: pl.BlockSpec((rows_blk, w), lambda b, j: (b * nblk + j, 0))
    bwd = lambda w: pl.BlockSpec((rows_blk, w), lambda b, j: (b * nblk + nblk - 1 - j, 0))
    per_b = lambda shape: pl.BlockSpec((1,) + shape, lambda b, j: (b,) + tuple(0 for _ in shape))
    const = lambda shape: pl.BlockSpec(shape, lambda b, j: tuple(0 for _ in shape))
    row_out = jax.ShapeDtypeStruct((rows, 256), F32)
    m_in_specs = [fwd(QKV_W), bwd(QKV_W), fwd(LANES), bwd(LANES), const((1, LANES)),
                  const((N_DIR, M_SEL_K, M_EXP_W)), per_b((N_DIR, 256, 512)), per_b((N_DIR, SUBLANES, LANES))]
    s_in_specs = [fwd(256), fwd(512), bwd(256), bwd(512), fwd(LANES), bwd(LANES), const((1, LANES)),
                  const((N_DIR, S_SEL_K, S_EXP_W)), const((SUBLANES, LANES)), const((1, 256)),
                  per_b((N_DIR, 256, 256))]
    m_out_specs = [fwd(256), bwd(256), per_b((N_DIR, 256, LANES)), per_b((N_DIR, SUBLANES, LANES))]
    s_out_specs = [fwd(256), bwd(256), per_b((N_DIR, 256, 256))]
    m_out_shape = [row_out, row_out, jax.ShapeDtypeStruct((batch, N_DIR, 256, LANES), F32),
                   jax.ShapeDtypeStruct((batch, N_DIR, SUBLANES, LANES), F32)]
    s_out_shape = [row_out, row_out, jax.ShapeDtypeStruct((batch, N_DIR, 256, 256), F32)]
    m_scratch = [pltpu.VMEM((N_DIR, 256, 512), F32), pltpu.VMEM((N_DIR, SUBLANES, LANES), F32),
                 pltpu.VMEM((N_DIR, rows_blk, M_EXP_W), F32), pltpu.VMEM((N_DIR, SUBLANES, HS), F32)]
    s_scratch = [pltpu.VMEM((N_DIR, 256, 256), F32), pltpu.VMEM((N_DIR, rows_blk, S_EXP_W), F32),
                 pltpu.VMEM((N_DIR, SUBLANES, HS), F32), pltpu.VMEM((N_DIR, SUBLANES, HS), F32)]
    assert nb <= NB_MAX
    assert (len(m_in_specs), len(m_out_specs), len(m_scratch)) == (N_M_IN, N_M_OUT, N_M_SCR)
    assert (len(s_in_specs), len(s_out_specs), len(s_scratch)) == (N_S_IN, N_S_OUT, N_S_SCR)
    m_args = [qkv, qkv, gates, gates, m_bias, m_sel, c0, m0]
    s_args = [xs, bc, xs, bc, gates, gates, dtb_row, s_sel, alog8, dskip_exp, h0]
    zero_state = c0 is None
    if zero_state:
        assert m0 is None and h0 is None
        m_in_specs, m_args = m_in_specs[:-2], m_args[:-2]
        s_in_specs, s_args = s_in_specs[:-1], s_args[:-1]
    return pl.pallas_call(
        functools.partial(_scan_kernel, nb=nb, zero_state=zero_state),
        grid=(batch, nblk),
        in_specs=m_in_specs + s_in_specs,
        out_specs=m_out_specs + s_out_specs,
        out_shape=m_out_shape + s_out_shape,
        scratch_shapes=m_scratch + s_scratch,
        compiler_params=pltpu.CompilerParams(
            dimension_semantics=("arbitrary", "arbitrary"), vmem_limit_bytes=VMEM_LIMIT),
        name="scans",
    )(*m_args, *s_args)


def _mix_kernel(x_ref, hf_ref, hb_ref, misc_ref, pprev_ref, pnext_ref, yf_ref, yb_ref, mod_ref,
                wout_ref, wpool_ref, pscale_ref, mnorm_ref, snorm_ref, wsp_ref, bsp_ref, ln_g_ref, ln_b_ref,
                mean_sel_ref, o_ref, cat_ref, pad_ref, *, seq, per_seq):
    rows = x_ref.shape[0]
    i = pl.program_id(0)
    hs = hf_ref[...] + hb_ref[...]
    mu = _dot_sel2(hs, mean_sel_ref[...])
    hc = hs - mu
    var = _dot_sel2(hc * hc, mean_sel_ref[...])
    y_m = _sigmoid(misc_ref[:, 0:256]) * (hc * lax.rsqrt(var + LN_EPS) * mnorm_ref[...])
    cat_ref[:, 0:256] = y_m.astype(BF16)
    blk = i % per_seq
    xp = misc_ref[:, 256:512]
    pad_ref[0:SUBLANES, :] = jnp.where(blk == 0, 0.0, pprev_ref[...])
    pad_ref[SUBLANES:SUBLANES + rows, :] = xp
    pad_ref[SUBLANES + rows:2 * SUBLANES + rows, :] = jnp.where(blk == per_seq - 1, 0.0, pnext_ref[...])
    n_pad = rows + 2 * SUBLANES
    back = lambda a, k: pltpu.roll(a, k, 0)
    ahead = lambda a, k: pltpu.roll(a, n_pad - k, 0)
    s2 = pad_ref[...]
    s2 = s2 + back(s2, 1)
    s4 = s2 + back(s2, 2)
    lo_half = _iota((n_pad, LANES), 1) < HD
    s4_hi = s4[:, LANES:2 * LANES]
    s8 = s4_hi + back(s4_hi, 4)
    s16 = s8 + back(s8, 8)
    wsum = jnp.concatenate([jnp.where(lo_half, s2[:, 0:LANES], ahead(s4[:, 0:LANES], 1)),
                            jnp.where(lo_half, ahead(s8, 3), ahead(s16, 7))], axis=1)
    wsum = wsum[SUBLANES:SUBLANES + rows, :]
    win = jnp.left_shift(2, _iota((SUBLANES, 256), 1) // HD)
    half = win // 2
    pooled = wsum * (1.0 / win[0:1, :].astype(F32)) - xp

    def edge(r0, t0):
        t = t0 + _iota((SUBLANES, 256), 0)
        cnt = jnp.clip(t - half + win, 0, seq) - jnp.clip(t - half, 0, seq)
        return wsum[r0:r0 + SUBLANES, :] / cnt.astype(F32) - xp[r0:r0 + SUBLANES, :]

    top = jnp.where(blk == 0, edge(0, 0), pooled[0:SUBLANES, :])
    bot = jnp.where(blk == per_seq - 1, edge(rows - SUBLANES, seq - SUBLANES), pooled[rows - SUBLANES:rows, :])
    pooled = jnp.concatenate([top, pooled[SUBLANES:rows - SUBLANES, :], bot], axis=0)
    y_p = _dot(pooled.astype(BF16), wpool_ref[...]) * pscale_ref[...]
    cat_ref[:, 256:512] = y_p.astype(BF16)
    vn = _ln(misc_ref[:, 768:1024]).astype(BF16)
    bd = (_iota((512, 256), 0) // G_CHUNK) == (_iota((512, 256), 1) // HD)
    for c in range(rows // G_CHUNK):
        r = slice(c * G_CHUNK, (c + 1) * G_CHUNK)
        vbd = jnp.where(bd, _tile4(vn[r, :]), jnp.zeros((), BF16))
        mixed = _dot(wsp_ref[...], vbd) + bsp_ref[...]
        cat_ref[r, 512:768] = (misc_ref[r, 512:768] * mixed).astype(BF16)
    y = (yf_ref[...] + yb_ref[...]) * _silu(misc_ref[:, 1024:1280])
    for g in range(2):
        yg = y[:, g * LANES:(g + 1) * LANES]
        yg = yg * lax.rsqrt(jnp.mean(yg * yg, -1, keepdims=True) + LN_EPS)
        cat_ref[:, 768 + g * LANES:768 + (g + 1) * LANES] = (yg * snorm_ref[:, g * LANES:(g + 1) * LANES]).astype(BF16)
    mix = _dot(cat_ref[...], wout_ref[...])
    gate = mod_ref[0, 2:3, :]
    o_ref[...] = _ln(ALPHA * x_ref[...] + gate * mix) * ln_g_ref[...] + ln_b_ref[...]


def _mix_call(x, hf, hb, misc, yf, yb, mod_l, w_out, w_pool_bd, pool_scale, mnorm, snorm, wsp_cat, bsp_exp,
              ln_g, ln_b, mean_sel, seq, rows_blk, cond_of_tile):
    rows = x.shape[0]
    nt = rows // rows_blk
    per_seq = seq // rows_blk
    sub_per_blk = rows_blk // SUBLANES
    n_sub = rows // SUBLANES
    row_spec = lambda w: pl.BlockSpec((rows_blk, w), lambda i: (i, 0))
    const = lambda shape: pl.BlockSpec(shape, lambda i: tuple(0 for _ in shape))
    return pl.pallas_call(
        functools.partial(_mix_kernel, seq=seq, per_seq=per_seq),
        grid=(nt,),
        in_specs=[row_spec(D_MODEL), row_spec(256), row_spec(256), row_spec(MISC_W),
                  pl.BlockSpec((SUBLANES, 256), lambda i: (jnp.maximum(i * sub_per_blk - 1, 0), 1)),
                  pl.BlockSpec((SUBLANES, 256), lambda i: (jnp.minimum((i + 1) * sub_per_blk, n_sub - 1), 1)),
                  row_spec(256), row_spec(256),
                  pl.BlockSpec((1, 6, D_MODEL), lambda i: (cond_of_tile(i), 0, 0)),
                  const((D_MODEL, D_MODEL)), const((256, 256)), const((1, 256)), const((1, 256)), const((1, 256)),
                  const((G_CHUNK, 512)), const((G_CHUNK, 256)), const((1, D_MODEL)), const((1, D_MODEL)),
                  const((256, 256))],
        out_specs=row_spec(D_MODEL),
        out_shape=jax.ShapeDtypeStruct((rows, D_MODEL), F32),
        scratch_shapes=[pltpu.VMEM((rows_blk, D_MODEL), BF16), pltpu.VMEM((rows_blk + 2 * SUBLANES, 256), F32)],
        compiler_params=pltpu.CompilerParams(dimension_semantics=("arbitrary",), vmem_limit_bytes=VMEM_LIMIT),
        name="mix_outproj",
    )(x, hf, hb, misc, misc, misc, yf, yb, mod_l, w_out, w_pool_bd, pool_scale, mnorm, snorm, wsp_cat, bsp_exp,
      ln_g, ln_b, mean_sel)


def _ffn_kernel(x_ref, xprev_ref, xnext_ref, mod_ref, wup_ref, cw_ref, cb_ref, wdown_ref, ln_g_ref, ln_b_ref,
                o_ref, h_ref, act_ref, *, per_seq):
    rows = x_ref.shape[0]
    blk = pl.program_id(0) % per_seq
    shift = mod_ref[0, 3:4, :]
    scale = mod_ref[0, 4:5, :]
    gate = mod_ref[0, 5:6, :]
    modulate = lambda v: (_ln(v) * (1.0 + scale) + shift).astype(BF16)
    x = x_ref[...]
    base = 2 * SUBLANES
    n_pad = rows + 2 * base
    hp = modulate(jnp.concatenate([xprev_ref[...], xprev_ref[...]], axis=0))
    hn = modulate(jnp.concatenate([xnext_ref[...], xnext_ref[...]], axis=0))
    h_ref[0:base, :] = jnp.where(blk > 0, hp, jnp.zeros((), BF16))
    h_ref[base:base + rows, :] = modulate(x)
    h_ref[base + rows:n_pad, :] = jnp.where(blk < per_seq - 1, hn, jnp.zeros((), BF16))
    h = h_ref[...]

    def conv(u, col):
        up = pltpu.roll(u, 1, 0)[base:base + rows, :]
        un = pltpu.roll(u, n_pad - 1, 0)[base:base + rows, :]
        return (cw_ref[0:1, col] * up + cw_ref[1:2, col] * u[base:base + rows, :] + cw_ref[2:3, col] * un
                + cb_ref[:, col])

    for c in range(D_FF // FF_CHUNK):
        cg = slice(c * FF_CHUNK, (c + 1) * FF_CHUNK)
        cv = slice(D_FF + c * FF_CHUNK, D_FF + (c + 1) * FF_CHUNK)
        g = conv(_dot(h, wup_ref[:, cg]), cg)
        val = conv(_dot(h, wup_ref[:, cv]), cv)
        act_ref[:, cg] = (_silu(g) * val).astype(BF16)
    ffn = _dot(act_ref[...], wdown_ref[...])
    o_ref[...] = _ln(ALPHA * x + gate * ffn) * ln_g_ref[...] + ln_b_ref[...]


def _ffn_call(x, mod_l, w_up, conv_w, conv_b, w_down, ln_g, ln_b, seq, rows_blk, cond_of_tile):
    rows = x.shape[0]
    nt = rows // rows_blk
    per_seq = seq // rows_blk
    sub_per_blk = rows_blk // SUBLANES
    n_sub = rows // SUBLANES
    row_spec = lambda w: pl.BlockSpec((rows_blk, w), lambda i: (i, 0))
    const = lambda shape: pl.BlockSpec(shape, lambda i: tuple(0 for _ in shape))
    return pl.pallas_call(
        functools.partial(_ffn_kernel, per_seq=per_seq),
        grid=(nt,),
        in_specs=[row_spec(D_MODEL),
                  pl.BlockSpec((SUBLANES, D_MODEL), lambda i: (jnp.maximum(i * sub_per_blk - 1, 0), 0)),
                  pl.BlockSpec((SUBLANES, D_MODEL), lambda i: (jnp.minimum((i + 1) * sub_per_blk, n_sub - 1), 0)),
                  pl.BlockSpec((1, 6, D_MODEL), lambda i: (cond_of_tile(i), 0, 0)),
                  const((D_MODEL, 2 * D_FF)), const((3, 2 * D_FF)), const((1, 2 * D_FF)),
                  const((D_FF, D_MODEL)), const((1, D_MODEL)), const((1, D_MODEL))],
        out_specs=row_spec(D_MODEL),
        out_shape=jax.ShapeDtypeStruct((rows, D_MODEL), F32),
        scratch_shapes=[pltpu.VMEM((rows_blk + 4 * SUBLANES, D_MODEL), BF16), pltpu.VMEM((rows_blk, D_FF), BF16)],
        compiler_params=pltpu.CompilerParams(dimension_semantics=("arbitrary",), vmem_limit_bytes=VMEM_LIMIT),
        name="conv_ffn",
    )(x, x, x, mod_l, w_up, conv_w, conv_b, w_down, ln_g, ln_b)


def _grid_pos_embed(rows, dim):
    quarter = dim // 4
    freq = 1.0 / (POS_BASE ** (jnp.arange(quarter, dtype=F32) / quarter))
    r = jnp.repeat(jnp.arange(rows, dtype=F32), GRID_W)
    col = jnp.tile(jnp.arange(GRID_W, dtype=F32), rows)

    def enc(pos):
        ang = pos[:, None] * freq[None, :]
        return jnp.concatenate([jnp.sin(ang), jnp.cos(ang)], -1)

    return jnp.concatenate([enc(r), enc(col)], -1)


def _selector(n_quant, pad_rows, widths):
    sel = np.zeros((N_DIR, 3 * pad_rows, sum(widths)), np.float32)
    offs = np.concatenate([[0], np.cumsum(widths)])
    for d in range(N_DIR):
        for p in range(3):
            for q in range(n_quant):
                per_head = widths[q] // HEADS
                for h in range(HEADS):
                    lo = offs[q] + h * per_head
                    sel[d, p * pad_rows + q * SUBLANES + d * HEADS + h, lo:lo + per_head] = 1.0
    return jnp.asarray(sel, BF16)


def _constants():
    m_sel = _selector(4, 32, [HS, 256, 256, 256])
    s_sel = _selector(3, 32, [HS, 256, 256])
    mean_sel = np.kron(np.eye(HEADS, dtype=np.float32), np.full((HD, HD), 1.0 / HD, np.float32))
    return m_sel, s_sel, jnp.asarray(mean_sel, BF16)


def _permute_w_in(w):
    wb = w.astype(BF16)
    pad = jnp.zeros((D_MODEL, LANES - 24), BF16)
    return jnp.concatenate([wb[:, 0:1024], wb[:, 1040:2832], wb[:, 1024:1040], wb[:, 2832:2840], pad], axis=1)


def _block_diag4(blocks):
    a, b = blocks.shape[1:]
    eye = jnp.eye(HEADS, dtype=blocks.dtype)
    return (eye[:, None, :, None] * blocks[:, :, None, :]).reshape(HEADS * a, HEADS * b)


def _dir_head_rows(v):
    rep = jnp.broadcast_to(v[..., None], v.shape + (LANES,))
    zeros = jnp.zeros_like(rep[..., 0, :, :])
    return jnp.stack([jnp.concatenate([rep[..., 0, :, :], zeros], axis=-2),
                      jnp.concatenate([zeros, rep[..., 1, :, :]], axis=-2)], axis=-3)


def _pack_mlstm_state(c, n, m):
    eye = jnp.eye(HEADS, dtype=F32)
    cbd = (eye[None, None, :, None, :, None] * c[:, :, :, :, None, :]).reshape(c.shape[0], N_DIR, 256, 256)
    nbd = jnp.broadcast_to((eye[None, None, :, None, :, None] * n[:, :, :, :, None, None]),
                           (c.shape[0], N_DIR, HEADS, HD, HEADS, HD)).reshape(c.shape[0], N_DIR, 256, 256)
    return jnp.concatenate([cbd, nbd], axis=-1), _dir_head_rows(m)


def _unpack_mlstm_state(cout, mout):
    b = cout.shape[0]
    c = cout[..., 0:HD].reshape(b, N_DIR, HEADS, HD, HD)
    n = cout[..., HD].reshape(b, N_DIR, HEADS, HD)
    m = jnp.stack([mout[:, d, d * HEADS:(d + 1) * HEADS, 0] for d in range(N_DIR)], axis=1)
    return c, n, m


def _pack_ssd_state(s):
    b = s.shape[0]
    st = jnp.transpose(s, (0, 1, 4, 2, 3))
    grp = (jnp.arange(HEADS) // 2)[None, :] == jnp.arange(2)[:, None]
    full = st[:, :, None, :, :, :] * grp[None, None, :, None, :, None].astype(F32)
    return full.reshape(b, N_DIR, 256, 256)


def _unpack_ssd_state(hp):
    heads = [jnp.swapaxes(hp[:, :, (h // 2) * S_STATE:(h // 2 + 1) * S_STATE, h * HD:(h + 1) * HD], -1, -2)
             for h in range(HEADS)]
    return jnp.stack(heads, axis=2)


def _layer(x, pos, mod_l, p, m_state, s_state, batch, seq, rows_blk, cond_of_tile, consts):
    m_sel, s_sel, mean_sel = consts
    outs = _inproj_call(x, pos, mod_l, p["w_in"], p["ssd_conv_w"], p["ssd_conv_b"], seq, rows_blk, cond_of_tile)
    if pos is not None:
        qkv, misc, xs, bc, gates, x = outs
    else:
        qkv, misc, xs, bc, gates = outs
    hf, hb, cout, mout, yf, yb, hs = _scan_call(qkv, xs, bc, gates, p["m_bias"], m_sel, m_state[0], m_state[1],
                                                p["dt_bias"], s_sel, p["a_log"], p["d_skip"], s_state, batch, seq,
                                                rows_blk)
    x1 = _mix_call(x, hf, hb, misc, yf, yb, mod_l, p["w_out"], p["w_pool"], p["pool_scale"], p["mnorm"],
                   p["snorm"], p["wsp"], p["bsp"], p["ln1_g"], p["ln1_b"], mean_sel, seq, rows_blk, cond_of_tile)
    x2 = _ffn_call(x1, mod_l, p["ffn_w_up"], p["ffn_conv_w"], p["ffn_conv_b"], p["ffn_w_down"], p["ln2_g"],
                   p["ln2_b"], seq, rows_blk, cond_of_tile)
    return x2, (cout, mout), hs


def kernel(x_prompt, x_sample, state_mlstm_c, state_mlstm_n, state_mlstm_m, state_ssd, c, c_ctx, w_ada, b_ada, w_in, b_igate, b_fgate, mlstm_norm_g, w_pool, pool_scale, w_spatial, b_spatial, ssd_conv_w, ssd_conv_b, ssd_dt_bias, ssd_a_log, ssd_d, ssd_norm_g, w_out, ln1_g, ln1_b, ffn_w_up, ffn_conv_w, ffn_conv_b, ffn_w_down, ln2_g, ln2_b):
    n_ctx, t_ctx, _ = x_prompt.shape
    n_lat, t_lat, _ = x_sample.shape
    consts = _constants()
    pos = _grid_pos_embed(t_lat // GRID_W, D_MODEL)

    cond = jnp.zeros((N_COND, D_MODEL), F32).at[0:n_lat].set(c).at[CTX_COND].set(c_ctx)
    mod = _ada_call(cond, w_ada, b_ada).reshape(DEPTH, N_COND, 6, D_MODEL)

    ctx_blk = min(t_ctx, 256)
    lat_blk = min(t_lat, 512)
    lat_per_seq = t_lat // lat_blk
    ctx_cond = lambda i: CTX_COND
    lat_cond = lambda i: i // lat_per_seq

    y_p = x_prompt.reshape(n_ctx * t_ctx, D_MODEL)
    y_s = x_sample.reshape(n_lat * t_lat, D_MODEL)
    zero_m = (None, None)
    zero_s = None
    new_c, new_n, new_m, new_s = [], [], [], []
    for l in range(DEPTH):
        m_bias = jnp.zeros((1, LANES), F32).at[0, 0:8].set(b_igate[l].reshape(-1)).at[0, 8:16].set(b_fgate[l].reshape(-1))
        dt_bias = jnp.zeros((1, LANES), F32).at[0, 16:24].set(ssd_dt_bias[l].reshape(-1))
        p = {
            "w_in": _permute_w_in(w_in[l]),
            "m_bias": m_bias,
            "dt_bias": dt_bias,
            "a_log": jnp.broadcast_to(ssd_a_log[l].reshape(N_DIR * HEADS, 1), (N_DIR * HEADS, LANES)),
            "d_skip": jnp.repeat(ssd_d[l], HD)[None, :],
            "ssd_conv_w": ssd_conv_w[l],
            "ssd_conv_b": ssd_conv_b[l][None, :],
            "w_out": w_out[l].astype(BF16),
            "w_pool": _block_diag4(w_pool[l]).astype(BF16),
            "pool_scale": pool_scale[l][None, :],
            "mnorm": mlstm_norm_g[l][None, :],
            "snorm": ssd_norm_g[l][None, :],
            "wsp": jnp.transpose(w_spatial[l], (1, 0, 2)).reshape(G_CHUNK, HEADS * G_CHUNK).astype(BF16),
            "bsp": jnp.repeat(b_spatial[l].T, HD, axis=-1),
            "ln1_g": ln1_g[l][None, :], "ln1_b": ln1_b[l][None, :],
            "ffn_w_up": ffn_w_up[l].astype(BF16),
            "ffn_conv_w": ffn_conv_w[l],
            "ffn_conv_b": ffn_conv_b[l][None, :],
            "ffn_w_down": ffn_w_down[l].astype(BF16),
            "ln2_g": ln2_g[l][None, :], "ln2_b": ln2_b[l][None, :],
        }
        y_p, (cout, mout), hs = _layer(y_p, None, mod[l], p, zero_m, zero_s, n_ctx, t_ctx, ctx_blk, ctx_cond, consts)
        cc, nn, mm = _unpack_mlstm_state(cout, mout)
        new_c.append(cc)
        new_n.append(nn)
        new_m.append(mm)
        new_s.append(_unpack_ssd_state(hs))
        lat_m = _pack_mlstm_state(state_mlstm_c[:, l], state_mlstm_n[:, l], state_mlstm_m[:, l])
        y_s, _, _ = _layer(y_s, pos if l == 0 else None, mod[l], p, lat_m, _pack_ssd_state(state_ssd[:, l]),
                           n_lat, t_lat, lat_blk, lat_cond, consts)
    return (y_p.reshape(n_ctx, t_ctx, D_MODEL), y_s.reshape(n_lat, t_lat, D_MODEL),
            jnp.stack(new_c, 1), jnp.stack(new_n, 1), jnp.stack(new_m, 1), jnp.stack(new_s, 1))
```
